```python
import functools
import jax, jax.numpy as jnp
from jax import lax
import numpy as np

D_MODEL = 1024
BATCH = 32
SEQ = 2048
DEPTH = 1
DEC_BATCH = 128
DEC_SEQ = 1
PAST_LEN = 8192
PAGE_SIZE = 128

N_HEADS = 8
N_KV_HEADS = 2
HEAD_DIM = 64
ATTN_WIDTH = N_HEADS * HEAD_DIM
KV_WIDTH = N_KV_HEADS * HEAD_DIM
IDX_HEADS = 4
IDX_DIM = 64
TOPK_MAX = 256
Q_BLOCK = 128
GDN_HEADS = 4
GDN_DK = 128
GDN_DV = 128
GDN_KW = GDN_HEADS * GDN_DK
GDN_VW = GDN_HEADS * GDN_DV
CONV_W = 4
CONV_DIM = 2 * GDN_KW + GDN_VW
GDN_CHUNK = 64
MIX_WIDTH = ATTN_WIDTH + GDN_VW
MEM_LEN = 256
MEM_HEADS = 4
MEM_HEAD_DIM = 64
MEM_WIDTH = MEM_HEADS * MEM_HEAD_DIM
D_FF = -(-8 * D_MODEL // (3 * 256)) * 256
IN_SPLITS = (ATTN_WIDTH, KV_WIDTH, KV_WIDTH, IDX_HEADS * IDX_DIM, IDX_DIM, IDX_HEADS,
             GDN_KW, GDN_KW, GDN_VW, GDN_VW, GDN_HEADS, GDN_HEADS)
SPLIT_POINTS = tuple(int(s) for s in np.cumsum(IN_SPLITS)[:-1])
IN_WIDTH = int(sum(IN_SPLITS))

kernel_name = 'hybrid_dsa_gdn_decoder_step'

F32 = jnp.float32


def rmsnorm(x, g, eps=1e-6):
    xf = x.astype(F32)
    y = xf * lax.rsqrt(jnp.mean(xf * xf, axis=-1, keepdims=True) + eps)
    return (y * g.astype(F32)).astype(x.dtype)


def l2norm(x, eps=1e-6):
    xf = x.astype(F32)
    return xf * lax.rsqrt(jnp.sum(xf * xf, axis=-1, keepdims=True) + eps)


def gather_rows(rows, idx):
    return jax.vmap(lambda r, i: r[i])(rows, idx)


def causal_conv(u, buf, conv_w):
    t = u.shape[1]
    full = jnp.concatenate([buf.astype(u.dtype), u], axis=1)
    y = full[:, 0:t] * conv_w[0]
    for j in range(1, CONV_W):
        y = y + full[:, j:j + t] * conv_w[j]
    return jax.nn.silu(y), full[:, t:]


def index_select(qi, wi, kidx, qpos, topk):
    n_keys = kidx.shape[1]
    rel = jax.nn.relu(jnp.einsum('bthd,bsd->bths', qi.astype(F32), kidx.astype(F32)) * IDX_DIM ** -0.5)
    score = jnp.einsum('bth,bths->bts', wi.astype(F32), rel)
    admissible = jnp.arange(n_keys)[None, :] <= qpos[:, None]
    score = jnp.where(admissible[None], score, -jnp.inf)
    _, idx = lax.top_k(score, topk)
    valid = idx <= qpos[None, :, None]
    return idx, valid


def sparse_attend(q, k_sel, v_sel, valid):
    b, t = q.shape[:2]
    qg = q.reshape(b, t, N_KV_HEADS, N_HEADS // N_KV_HEADS, HEAD_DIM)
    logits = jnp.einsum('btgrd,btsgd->btgrs', qg.astype(F32), k_sel.astype(F32)) * HEAD_DIM ** -0.5
    logits = jnp.where(valid[:, :, None, None, :], logits, -jnp.inf)
    p = jax.nn.softmax(logits, axis=-1)
    o = jnp.einsum('btgrs,btsgd->btgrd', p, v_sel.astype(F32))
    return o.reshape(b, t, ATTN_WIDTH).astype(q.dtype)


def dsa_prompt(q, k, v, qi, wi, ki):
    b, t = q.shape[:2]
    topk = min(TOPK_MAX, t // 4)
    nblk = t // Q_BLOCK

    def blockify(a):
        return jnp.moveaxis(a.reshape((b, nblk, Q_BLOCK) + a.shape[2:]), 1, 0)

    def one_block(args):
        blk, qb, qib, wib = args
        qpos = blk * Q_BLOCK + jnp.arange(Q_BLOCK)
        idx, valid = index_select(qib, wib, ki, qpos, topk)
        return sparse_attend(qb, gather_rows(k, idx), gather_rows(v, idx), valid)

    out = lax.map(one_block, (jnp.arange(nblk), blockify(q), blockify(qi), blockify(wi)))
    return jnp.moveaxis(out, 0, 1).reshape(b, t, ATTN_WIDTH)


def dsa_sample(q, k, v, qi, wi, ki, cache_k, cache_v, cache_kidx, page_table):
    b, t = q.shape[:2]
    past = page_table.shape[1] * PAGE_SIZE
    topk = min(TOPK_MAX, (past + t) // 4)
    kidx_past = cache_kidx[page_table].reshape(b, past, IDX_DIM)
    kidx_all = jnp.concatenate([kidx_past, ki.astype(kidx_past.dtype)], axis=1)
    qpos = past + jnp.arange(t)
    idx, valid = index_select(qi, wi, kidx_all, qpos, topk)
    in_past = (idx < past)[..., None, None]
    pidx = jnp.minimum(idx, past - 1)
    phys = gather_rows(page_table, pidx // PAGE_SIZE)
    off = pidx % PAGE_SIZE
    nidx = jnp.clip(idx - past, 0, t - 1)
    k_sel = jnp.where(in_past, cache_k[phys, off], gather_rows(k, nidx).astype(cache_k.dtype))
    v_sel = jnp.where(in_past, cache_v[phys, off], gather_rows(v, nidx).astype(cache_v.dtype))
    return sparse_attend(q, k_sel, v_sel, valid)


def gated_delta_chunked(q, k, v, g, beta, s0):
    b, t, h, dk = q.shape
    c = min(GDN_CHUNK, t)
    n = -(-t // c)
    pad = n * c - t

    def prep(a):
        a = jnp.pad(a, [(0, 0), (0, pad)] + [(0, 0)] * (a.ndim - 2))
        return jnp.moveaxis(a.reshape((b, n, c) + a.shape[2:]), 3, 1)

    q, k, v, g, beta = (prep(a) for a in (q * dk ** -0.5, k, v, g, beta))
    gc = jnp.cumsum(g, axis=-1)
    tril = jnp.tril(jnp.ones((c, c), bool))
    strict = jnp.tril(jnp.ones((c, c), bool), -1)
    diff = gc[..., :, None] - gc[..., None, :]
    decay = jnp.where(tril, jnp.exp(jnp.where(tril, diff, 0.0)), 0.0)
    kb = k * beta[..., None]
    vb = v * beta[..., None]
    a_mat = jnp.where(strict, jnp.einsum('bhncd,bhnsd->bhncs', kb, k) * decay, 0.0)
    eye = jnp.eye(c, dtype=F32)
    tinv = lax.linalg.triangular_solve(eye + a_mat, jnp.broadcast_to(eye, a_mat.shape),
                                       left_side=True, lower=True, unit_diagonal=True)
    u = jnp.einsum('bhncs,bhnsd->bhncd', tinv, vb)
    w = jnp.einsum('bhncs,bhnsd->bhncd', tinv, kb * jnp.exp(gc)[..., None])
    qk = jnp.where(tril, jnp.einsum('bhncd,bhnsd->bhncs', q, k) * decay, 0.0)

    def step(s, xs):
        q_i, k_i, u_i, w_i, qk_i, gc_i = xs
        v_new = u_i - jnp.einsum('bhcd,bhde->bhce', w_i, s)
        o = (jnp.einsum('bhcd,bhde->bhce', q_i * jnp.exp(gc_i)[..., None], s)
             + jnp.einsum('bhcs,bhse->bhce', qk_i, v_new))
        g_last = gc_i[..., -1]
        s = (s * jnp.exp(g_last)[..., None, None]
             + jnp.einsum('bhcd,bhce->bhde', k_i * jnp.exp(g_last[..., None] - gc_i)[..., None], v_new))
        return s, o

    xs = tuple(jnp.moveaxis(a, 2, 0) for a in (q, k, u, w, qk, gc))
    s_fin, o = lax.scan(step, s0, xs)
    o = jnp.transpose(o, (1, 0, 3, 2, 4)).reshape(b, n * c, h, o.shape[-1])[:, :t]
    return o, s_fin


def mem_kv(mem, mem_norm_g, w_mk, w_mv, mk_norm_g):
    b, m, _ = mem.shape
    hm = rmsnorm(mem, mem_norm_g)
    mk = rmsnorm((hm @ w_mk).reshape(b, m, MEM_HEADS, MEM_HEAD_DIM), mk_norm_g)
    mv = (hm @ w_mv).reshape(b, m, MEM_HEADS, MEM_HEAD_DIM)
    return mk, mv


def mem_attend(h, mem_k, mem_v, w_mq, mq_norm_g, w_mo):
    b, t, _ = h.shape
    q = rmsnorm((h @ w_mq).reshape(b, t, MEM_HEADS, MEM_HEAD_DIM), mq_norm_g)
    logits = jnp.einsum('bthd,bmhd->bhtm', q.astype(F32), mem_k.astype(F32)) * MEM_HEAD_DIM ** -0.5
    p = jax.nn.softmax(logits, axis=-1)
    o = jnp.einsum('bhtm,bmhd->bthd', p, mem_v.astype(F32))
    return o.reshape(b, t, MEM_WIDTH).astype(h.dtype) @ w_mo


def trunk_layer(x, mem_k, mem_v, conv_buf, ssm0, dsa_fn,
                attn_norm_g, w_in, q_norm_g, k_norm_g, conv_w, a_log, dt_bias, gdn_norm_g, w_out,
                xattn_norm_g, w_mq, mq_norm_g, w_mo, ffn_norm_g, w_gate, w_up, w_down):
    b, t, _ = x.shape
    h = rmsnorm(x, attn_norm_g)
    (q, k, v, qi, ki, wi, gq, gk, gv, gz, ga, gb) = jnp.split(h @ w_in, SPLIT_POINTS, axis=-1)
    q = rmsnorm(q.reshape(b, t, N_HEADS, HEAD_DIM), q_norm_g)
    k = rmsnorm(k.reshape(b, t, N_KV_HEADS, HEAD_DIM), k_norm_g)
    v = v.reshape(b, t, N_KV_HEADS, HEAD_DIM)
    qi = qi.reshape(b, t, IDX_HEADS, IDX_DIM)
    wi = wi * IDX_HEADS ** -0.5
    o_attn = dsa_fn(q, k, v, qi, wi, ki)
    conv_out, conv_new = causal_conv(jnp.concatenate([gq, gk, gv], axis=-1), conv_buf, conv_w)
    cq, ck, cv = jnp.split(conv_out, [GDN_KW, 2 * GDN_KW], axis=-1)
    cq = l2norm(cq.reshape(b, t, GDN_HEADS, GDN_DK))
    ck = l2norm(ck.reshape(b, t, GDN_HEADS, GDN_DK))
    cv = cv.reshape(b, t, GDN_HEADS, GDN_DV).astype(F32)
    beta = jax.nn.sigmoid(gb.astype(F32))
    g = -jnp.exp(a_log.astype(F32)) * jax.nn.softplus(ga.astype(F32) + dt_bias.astype(F32))
    o_gdn, ssm_new = gated_delta_chunked(cq, ck, cv, g, beta, ssm0.astype(F32))
    o_gdn = rmsnorm(o_gdn, gdn_norm_g) * jax.nn.silu(gz.reshape(b, t, GDN_HEADS, GDN_DV).astype(F32))
    o_gdn = o_gdn.reshape(b, t, GDN_VW).astype(x.dtype)
    x = x + jnp.concatenate([o_attn, o_gdn], axis=-1) @ w_out
    x = x + mem_attend(rmsnorm(x, xattn_norm_g), mem_k, mem_v, w_mq, mq_norm_g, w_mo)
    hf = rmsnorm(x, ffn_norm_g)
    x = x + (jax.nn.silu(hf @ w_gate) * (hf @ w_up)) @ w_down
    return x, (k, v, ki, conv_new, ssm_new.astype(ssm0.dtype))


def setup_inputs(seed: int = 0) -> dict:
    key = jax.random.key(seed)
    keys = iter(jax.random.split(key, 48))

    def nrm(shape, scale=1.0):
        return scale * jax.random.normal(next(keys), shape, jnp.float32)

    def gain(n):
        return 1.0 + nrm((DEPTH, n), 0.02)

    n_pages = PAST_LEN // PAGE_SIZE
    n_pool = (DEC_BATCH * n_pages * 5) // 4
    perm = jax.random.permutation(next(keys), n_pool)
    page_table = perm[:DEC_BATCH * n_pages].reshape(DEC_BATCH, n_pages).astype(jnp.int32)
    a_log = jnp.log(jax.random.uniform(next(keys), (DEPTH, GDN_HEADS), jnp.float32, 1.0, 16.0))
    return {
        'x_prompt': nrm((BATCH, SEQ, D_MODEL)),
        'x_sample': nrm((DEC_BATCH, DEC_SEQ, D_MODEL)),
        'mem_prompt': nrm((BATCH, MEM_LEN, D_MODEL)),
        'cache_k': nrm((DEPTH, n_pool, PAGE_SIZE, N_KV_HEADS, HEAD_DIM)),
        'cache_v': nrm((DEPTH, n_pool, PAGE_SIZE, N_KV_HEADS, HEAD_DIM)),
        'cache_kidx': nrm((DEPTH, n_pool, PAGE_SIZE, IDX_DIM)),
        'page_table': page_table,
        'state_conv': nrm((DEPTH, DEC_BATCH, CONV_W - 1, CONV_DIM)),
        'state_ssm': nrm((DEPTH, DEC_BATCH, GDN_HEADS, GDN_DK, GDN_DV), 0.1),
        'cache_mem_k': nrm((DEPTH, DEC_BATCH, MEM_LEN, MEM_HEADS, MEM_HEAD_DIM)),
        'cache_mem_v': nrm((DEPTH, DEC_BATCH, MEM_LEN, MEM_HEADS, MEM_HEAD_DIM)),
        'attn_norm_g': gain(D_MODEL),
        'w_in': nrm((DEPTH, D_MODEL, IN_WIDTH), D_MODEL ** -0.5),
        'q_norm_g': gain(HEAD_DIM),
        'k_norm_g': gain(HEAD_DIM),
        'conv_w': nrm((DEPTH, CONV_W, CONV_DIM), CONV_W ** -0.5),
        'a_log': a_log,
        'dt_bias': nrm((DEPTH, GDN_HEADS), 0.1) - 4.0,
        'gdn_norm_g': gain(GDN_DV),
        'w_out': nrm((DEPTH, MIX_WIDTH, D_MODEL), MIX_WIDTH ** -0.5),
        'xattn_norm_g': gain(D_MODEL),
        'mem_norm_g': gain(D_MODEL),
        'w_mq': nrm((DEPTH, D_MODEL, MEM_WIDTH), D_MODEL ** -0.5),
        'w_mk': nrm((DEPTH, D_MODEL, MEM_WIDTH), D_MODEL ** -0.5),
        'w_mv': nrm((DEPTH, D_MODEL, MEM_WIDTH), D_MODEL ** -0.5),
        'mq_norm_g': gain(MEM_HEAD_DIM),
        'mk_norm_g': gain(MEM_HEAD_DIM),
        'w_mo': nrm((DEPTH, MEM_WIDTH, D_MODEL), MEM_WIDTH ** -0.5),
        'ffn_norm_g': gain(D_MODEL),
        'w_gate': nrm((DEPTH, D_MODEL, D_FF), D_MODEL ** -0.5),
        'w_up': nrm((DEPTH, D_MODEL, D_FF), D_MODEL ** -0.5),
        'w_down': nrm((DEPTH, D_FF, D_MODEL), D_FF ** -0.5),
    }


def reference(x_prompt, x_sample, mem_prompt, cache_k, cache_v, cache_kidx, page_table,
              state_conv, state_ssm, cache_mem_k, cache_mem_v,
              attn_norm_g, w_in, q_norm_g, k_norm_g, conv_w, a_log, dt_bias, gdn_norm_g, w_out,
              xattn_norm_g, mem_norm_g, w_mq, w_mk, w_mv, mq_norm_g, mk_norm_g, w_mo,
              ffn_norm_g, w_gate, w_up, w_down):
    layer_w = (attn_norm_g, w_in, q_norm_g, k_norm_g, conv_w, a_log, dt_bias, gdn_norm_g, w_out,
               xattn_norm_g, w_mq, mq_norm_g, w_mo, ffn_norm_g, w_gate, w_up, w_down)
    xp, xs = x_prompt, x_sample
    new_p, new_s = [], []
    for l in range(DEPTH):
        wl = [a[l] for a in layer_w]
        mk_p, mv_p = mem_kv(mem_prompt, mem_norm_g[l], w_mk[l], w_mv[l], mk_norm_g[l])
        conv0 = jnp.zeros((xp.shape[0], CONV_W - 1, CONV_DIM), xp.dtype)
        ssm0 = jnp.zeros((xp.shape[0], GDN_HEADS, GDN_DK, GDN_DV), xp.dtype)
        xp, st_p = trunk_layer(xp, mk_p, mv_p, conv0, ssm0, dsa_prompt, *wl)
        dsa_s = functools.partial(dsa_sample, cache_k=cache_k[l], cache_v=cache_v[l],
                                  cache_kidx=cache_kidx[l], page_table=page_table)
        xs, st_s = trunk_layer(xs, cache_mem_k[l], cache_mem_v[l], state_conv[l], state_ssm[l], dsa_s, *wl)
        new_p.append(st_p + (mk_p, mv_p))
        new_s.append(st_s)
    k_prompt, v_prompt, kidx_prompt, conv_prompt, ssm_prompt, memk_prompt, memv_prompt = [
        jnp.stack(z) for z in zip(*new_p)]
    k_sample, v_sample, kidx_sample, conv_sample, ssm_sample = [jnp.stack(z) for z in zip(*new_s)]
    return (xp, xs, k_prompt, v_prompt, kidx_prompt, conv_prompt, ssm_prompt, memk_prompt, memv_prompt,
            k_sample, v_sample, kidx_sample, conv_sample, ssm_sample)
```

```python
import functools

import jax
import jax.numpy as jnp
from jax import lax
from jax.experimental import pallas as pl
from jax.experimental.pallas import tpu as pltpu

F32 = jnp.float32
MXU_DTYPE = jnp.bfloat16

D_MODEL = 1024
PAGE = 128
N_HEADS, N_KV, HEAD_DIM = 8, 2, 64
ATTN_W, KV_W = N_HEADS * HEAD_DIM, N_KV * HEAD_DIM
IDX_HEADS, IDX_DIM = 4, 64
TOPK_MAX = 256
G_HEADS, G_DK, G_DV = 4, 128, 128
G_KW, G_VW = G_HEADS * G_DK, G_HEADS * G_DV
CONV_W = 4
CONV_DIM = 2 * G_KW + G_VW
GDN_CHUNK = 64
MEM_HEADS, MEM_HD = 4, 64
MEM_W = MEM_HEADS * MEM_HD
DSA_COLS = ATTN_W + 2 * KV_W + IDX_HEADS * IDX_DIM + IDX_DIM + IDX_HEADS
DSA_PAD = 1152
GDN_COLS = 2 * G_KW + 2 * G_VW + 2 * G_HEADS
GDN_PAD = 2176
EPS = 1e-6
NEG = -1e30
INT_MIN = -2 ** 31
KEY_NEG_INF = -2139095041
FLT_MAX = 3.4028234663852886e38
VMEM_LIMIT = 56 * 1024 * 1024
ROW_TILE = 512
DSA_Q_BLOCK = 256
SAMPLE_PAGES_PER_STEP = 16


def _cparams(*sem):
    return pltpu.CompilerParams(dimension_semantics=sem, vmem_limit_bytes=VMEM_LIMIT)


def _mx(a):
    return a.astype(MXU_DTYPE)


def _dot(a, b):
    return jnp.dot(_mx(a), _mx(b), preferred_element_type=F32)


def _dot_nt(a, b):
    return lax.dot_general(_mx(a), _mx(b), (((1,), (1,)), ((), ())), preferred_element_type=F32)


def _bdot(a, b, ca, cb):
    return lax.dot_general(_mx(a), _mx(b), (((ca,), (cb,)), ((0,), (0,))), preferred_element_type=F32)


def _split3(a):
    hi = _mx(a)
    r1 = a - hi.astype(F32)
    mid = _mx(r1)
    lo = _mx(r1 - mid.astype(F32))
    return hi, mid, lo


def _iota2(shape, dim):
    return lax.broadcasted_iota(jnp.int32, shape, dim)


def _tri(n, kind):
    r, c = _iota2((n, n), 0), _iota2((n, n), 1)
    return {"upper": r <= c, "lower": r >= c, "strict_lower": r > c, "eye": r == c}[kind]


def _group_mean_matrix(n, group):
    shift = group.bit_length() - 1
    r, c = _iota2((n, n), 0) >> shift, _iota2((n, n), 1) >> shift
    return jnp.where(r == c, 1.0 / group, 0.0).astype(MXU_DTYPE)


def _silu(x):
    return x * (1.0 / (1.0 + jnp.exp(-x)))


def _key_to_float(key):
    bits = jnp.where(key < 0, key ^ jnp.int32(0x7FFFFFFF), key)
    return jnp.where(key < KEY_NEG_INF, -jnp.inf, lax.bitcast_convert_type(bits, F32))


def _next_candidate(it, base):
    return jnp.where(it == 0, jnp.zeros_like(base), base | lax.shift_left(jnp.int32(1), 31 - it))


def _proj_in_kernel(x_ref, g_ref, wd_ref, wg_ref, qg_ref, kg_ref,
                    q_ref, k_ref, v_ref, kiwi_ref, kvi_ref, qi_ref, cin_ref, gz_ref, gab_ref):
    x = x_ref[...]
    h = _mx(x * lax.rsqrt(jnp.mean(x * x, axis=-1, keepdims=True) + EPS) * g_ref[...])
    pd = jnp.dot(h, wd_ref[...], preferred_element_type=F32)
    q = pd[:, :ATTN_W]
    k = pd[:, ATTN_W:ATTN_W + KV_W]
    v = pd[:, ATTN_W + KV_W:ATTN_W + 2 * KV_W]
    qi = pd[:, ATTN_W + 2 * KV_W:ATTN_W + 2 * KV_W + IDX_HEADS * IDX_DIM]
    kiwi = pd[:, 1024:1152]
    m64 = _group_mean_matrix(ATTN_W, HEAD_DIM)
    q = q * lax.rsqrt(_dot(q * q, m64) + EPS) * qg_ref[...]
    k = k * lax.rsqrt(_dot(k * k, m64[:KV_W, :KV_W]) + EPS) * kg_ref[...]
    q_ref[...] = (q * HEAD_DIM ** -0.5).astype(q_ref.dtype)
    k_ref[...] = k
    v_ref[...] = v
    kiwi_ref[...] = kiwi
    kvi_ref[:, 0:KV_W] = k.astype(kvi_ref.dtype)
    kvi_ref[:, KV_W:2 * KV_W] = v.astype(kvi_ref.dtype)
    kvi_ref[:, 2 * KV_W:3 * KV_W] = kiwi.astype(kvi_ref.dtype)
    qi_ref[...] = (qi * IDX_DIM ** -0.5).astype(qi_ref.dtype)
    pg = jnp.dot(h, wg_ref[...], preferred_element_type=F32)
    cin_ref[...] = pg[:, :CONV_DIM]
    gz_ref[...] = pg[:, CONV_DIM:CONV_DIM + G_VW]
    gab_ref[...] = pg[:, CONV_DIM + G_VW:GDN_PAD]


def _proj_in(x2d, g, w_dsa, w_gdn, qg, kg, tm):
    n = x2d.shape[0]
    row = lambda w: pl.BlockSpec((tm, w), lambda i: (i, 0))
    full = lambda a: pl.BlockSpec(a.shape, lambda i: (0,) * a.ndim)
    outs = [(ATTN_W, MXU_DTYPE), (KV_W, F32), (KV_W, F32), (128, F32), (3 * KV_W, MXU_DTYPE),
            (IDX_HEADS * IDX_DIM, MXU_DTYPE), (CONV_DIM, F32), (G_VW, F32), (128, F32)]
    return pl.pallas_call(
        _proj_in_kernel,
        grid=(n // tm,),
        in_specs=[row(D_MODEL), full(g), full(w_dsa), full(w_gdn), full(qg), full(kg)],
        out_specs=[row(w) for w, _ in outs],
        out_shape=[jax.ShapeDtypeStruct((n, w), dt) for w, dt in outs],
        compiler_params=_cparams("parallel"),
        name="proj_in",
    )(x2d, g, w_dsa, w_gdn, qg, kg)


def _dsa_prompt_kernel(q_ref, qi_ref, kiwi_ref, kvi_ref, o_ref, s_ref, bias_ref, *, topk, tq, nkeys, row0):
    ki = kvi_ref[:, 2 * KV_W:2 * KV_W + IDX_DIM]
    wi = kiwi_ref[:, IDX_DIM:IDX_DIM + IDX_HEADS] * IDX_HEADS ** -0.5
    score = None
    for h in range(IDX_HEADS):
        rel = jnp.maximum(_dot_nt(qi_ref[:, IDX_DIM * h:IDX_DIM * (h + 1)], ki), 0.0)
        term = wi[:, h:h + 1] * rel
        score = term if score is None else score + term
    adm = _iota2((tq, nkeys), 1) <= _iota2((tq, nkeys), 0) + row0
    s_ref[...] = jnp.where(adm, score, -jnp.inf)
    _topk_bias(s_ref, bias_ref, topk, tq, nkeys)
    for g in range(N_KV):
        kg = kvi_ref[:, HEAD_DIM * g:HEAD_DIM * (g + 1)]
        vg = kvi_ref[:, KV_W + HEAD_DIM * g:KV_W + HEAD_DIM * (g + 1)]
        for r in range(N_HEADS // N_KV):
            h = g * (N_HEADS // N_KV) + r
            sc = _dot_nt(q_ref[:, HEAD_DIM * h:HEAD_DIM * (h + 1)], kg) + bias_ref[...]
            p = jnp.exp(sc - jnp.max(sc, axis=1, keepdims=True))
            l = jnp.sum(p, axis=1, keepdims=True)
            o_ref[:, HEAD_DIM * h:HEAD_DIM * (h + 1)] = (_dot(p, vg) / l).astype(o_ref.dtype)


def _topk_bias(s_ref, bias_ref, topk, tq, nkeys):
    halves = 2
    hr = tq // halves

    def body(it, bases):
        out = []
        for p, base in enumerate(bases):
            cand = _next_candidate(it, base)
            hit = s_ref[p * hr:(p + 1) * hr, :] >= _key_to_float(cand)
            cnt = jnp.sum(jnp.where(hit, 1.0, 0.0), axis=1, keepdims=True)
            out.append(jnp.where(cnt >= float(topk), cand, base))
        return tuple(out)

    bases = lax.fori_loop(0, 32, body, tuple(jnp.full((hr, 1), INT_MIN, jnp.int32) for _ in range(halves)))
    thr = jnp.maximum(_key_to_float(jnp.concatenate(bases, axis=0)), -FLT_MAX)
    s = s_ref[...]
    gt = s > thr
    tie = s == thr
    need = float(topk) - jnp.sum(jnp.where(gt, 1.0, 0.0), axis=1, keepdims=True)
    n_tie = jnp.sum(jnp.where(tie, 1.0, 0.0), axis=1, keepdims=True)
    crowded = jnp.max(n_tie - need) > 0.0

    @pl.when(jnp.logical_not(crowded))
    def _():
        bias_ref[...] = jnp.where(s >= thr, 0.0, NEG)

    @pl.when(crowded)
    def _():
        tie_f = _mx(jnp.where(tie, 1.0, 0.0))
        upper = _mx(jnp.where(_tri(128, "upper"), 1.0, 0.0))
        off = jnp.zeros((tq, 1), F32)
        parts = []
        for c in range(nkeys // 128):
            pc = jnp.dot(tie_f[:, 128 * c:128 * (c + 1)], upper, preferred_element_type=F32) + off
            off = pc[:, 127:128]
            parts.append(pc)
        rank = jnp.concatenate(parts, axis=1)
        bias_ref[...] = jnp.where(gt | (tie & (rank <= need)), 0.0, NEG)


def _dsa_prompt(q, qi, kiwi, kvi, b, t):
    tq = min(DSA_Q_BLOCK, t)
    topk = min(TOPK_MAX, t // 4)
    q, qi, kiwi, kvi = (a.reshape(b, t, a.shape[-1]) for a in (q, qi, kiwi, kvi))
    outs = []
    for i in range(t // tq):
        nkeys = (i + 1) * tq
        blk = lambda w, i=i: pl.BlockSpec((None, tq, w), lambda bb: (bb, i, 0))
        outs.append(pl.pallas_call(
            functools.partial(_dsa_prompt_kernel, topk=topk, tq=tq, nkeys=nkeys, row0=i * tq),
            grid=(b,),
            in_specs=[blk(ATTN_W), blk(IDX_HEADS * IDX_DIM), blk(128),
                      pl.BlockSpec((None, nkeys, 3 * KV_W), lambda bb: (bb, 0, 0))],
            out_specs=pl.BlockSpec((None, tq, ATTN_W), lambda bb: (bb, 0, 0)),
            out_shape=jax.ShapeDtypeStruct((b, tq, ATTN_W), MXU_DTYPE),
            scratch_shapes=[pltpu.VMEM((tq, nkeys), F32), pltpu.VMEM((tq, nkeys), F32)],
            compiler_params=_cparams("parallel"),
            name=f"dsa_prompt_{i}",
        )(q, qi, kiwi, kvi))
    return jnp.concatenate(outs, axis=1).reshape(b * t, ATTN_W)


def _dsa_score_kernel(pt_ref, qi_ref, wib_ref, kis_ref, *rest, npg, n_pages):
    pages = rest[:npg]
    sc_ref = rest[npg]
    j = pl.program_id(1)
    qi = qi_ref[...]
    w = wib_ref[...][:, 0:1]
    z = jnp.concatenate([_dot(qi, pages[i][...]) for i in range(npg)], axis=1)
    sc = jnp.sum(w * jnp.maximum(z, 0.0), axis=0, keepdims=True)
    for i in range(npg):
        sc_ref[pl.ds(j * npg + i, 1), :] = sc[:, PAGE * i:PAGE * (i + 1)]

    @pl.when(j == n_pages // npg - 1)
    def _():
        z_self = jnp.sum(qi.astype(F32) * _mx(kis_ref[...]).astype(F32), axis=1, keepdims=True)
        s_self = jnp.sum(w * jnp.maximum(z_self, 0.0), axis=0, keepdims=True)
        sc_ref[n_pages:n_pages + 1, :] = jnp.where(_iota2((1, 128), 1) == 0, s_self, -jnp.inf)


def _dsa_pick_kernel(s_ref, bias_ref, *, topk, rows, nkeys):
    _topk_bias(s_ref, bias_ref, topk, rows, nkeys)


def _dsa_att_kernel(pt_ref, qb_ref, bias_ref, kself_ref, vself_ref, *rest, npg, n_pages):
    kp = rest[:npg]
    vp = rest[npg:2 * npg]
    o_ref, m_ref, l_ref, acc_ref = rest[2 * npg:]
    j = pl.program_id(1)

    @pl.when(j == 0)
    def _():
        m_ref[...] = jnp.full(m_ref.shape, NEG, F32)
        l_ref[...] = jnp.zeros(l_ref.shape, F32)
        acc_ref[...] = jnp.zeros(acc_ref.shape, F32)

    qb = qb_ref[...]
    valid = jnp.concatenate([bias_ref[pl.ds(j * npg + i, 1), :] for i in range(npg)], axis=1) > -1.0
    s = jnp.concatenate([_dot(qb, kp[i][...]) for i in range(npg)], axis=1)
    s = jnp.where(valid, s, NEG)
    m_old = m_ref[...]
    m_new = jnp.maximum(m_old, jnp.max(s, axis=1, keepdims=True))
    alpha = jnp.exp(m_old - m_new)
    p = jnp.where(valid, jnp.exp(s - m_new), 0.0)
    pv = _dot_nt(p[:, 0:PAGE], vp[0][...])
    for i in range(1, npg):
        pv = pv + _dot_nt(p[:, PAGE * i:PAGE * (i + 1)], vp[i][...])
    l_ref[...] = alpha * l_ref[...] + jnp.sum(p, axis=1, keepdims=True)
    acc_ref[...] = alpha * acc_ref[...] + pv
    m_ref[...] = m_new

    @pl.when(j == n_pages // npg - 1)
    def _():
        self_ok = bias_ref[n_pages:n_pages + 1, 0:1] > -1.0
        s_self = jnp.sum(qb.astype(F32) * _mx(kself_ref[...]).astype(F32), axis=1, keepdims=True)
        s_self = jnp.where(self_ok, s_self, NEG)
        m_fin = jnp.maximum(m_new, s_self)
        a_fin = jnp.exp(m_new - m_fin)
        p_self = jnp.where(self_ok, jnp.exp(s_self - m_fin), 0.0)
        l = a_fin * l_ref[...] + p_self
        acc = a_fin * acc_ref[...] + _mx(p_self).astype(F32) * _mx(vself_ref[...]).astype(F32)
        o_ref[...] = (acc / l).astype(o_ref.dtype)


def _dsa_sample(q, qi, kiwi, k_new, v_new, cache_k, cache_v, cache_kidx, page_table):
    b, n_pages = page_table.shape
    n_pool = cache_k.shape[0]
    past = n_pages * PAGE
    topk = min(TOPK_MAX, (past + 1) // 4)
    npg = SAMPLE_PAGES_PER_STEP if n_pages % SAMPLE_PAGES_PER_STEP == 0 else n_pages
    pt = page_table.reshape(-1)
    ck = jnp.transpose(cache_k, (0, 2, 3, 1)).reshape(n_pool, KV_W, PAGE)
    cv = jnp.transpose(cache_v, (0, 2, 3, 1)).reshape(n_pool, KV_W, PAGE)
    cidx = jnp.transpose(cache_kidx, (0, 2, 1))
    qi8 = jnp.pad(qi.reshape(b, IDX_HEADS, IDX_DIM), ((0, 0), (0, 8 - IDX_HEADS), (0, 0)))
    wib = jnp.pad(kiwi[:, IDX_DIM:IDX_DIM + IDX_HEADS] * IDX_HEADS ** -0.5, ((0, 0), (0, 8 - IDX_HEADS)))
    wib = jnp.broadcast_to(wib[:, :, None], (b, 8, 128))
    kis = kiwi[:, None, :IDX_DIM]
    per_seq = lambda *shape: pl.BlockSpec((None,) + shape, lambda s, j, pt_: (s,) + (0,) * len(shape))
    page = lambda w, i: pl.BlockSpec((None, w, PAGE), lambda s, j, pt_, i=i: (pt_[s * n_pages + j * npg + i], 0, 0))
    scores = pl.pallas_call(
        functools.partial(_dsa_score_kernel, npg=npg, n_pages=n_pages),
        grid_spec=pltpu.PrefetchScalarGridSpec(
            num_scalar_prefetch=1,
            grid=(b, n_pages // npg),
            in_specs=[per_seq(8, IDX_DIM), per_seq(8, 128), per_seq(1, IDX_DIM)]
                     + [page(IDX_DIM, i) for i in range(npg)],
            out_specs=per_seq(n_pages + 1, 128)),
        out_shape=jax.ShapeDtypeStruct((b, n_pages + 1, 128), F32),
        compiler_params=_cparams("parallel", "arbitrary"),
        name="dsa_sample_score",
    )(pt, qi8, wib, kis, *([cidx] * npg))
    nkeys = (n_pages + 1) * PAGE
    rows = min(b, 64)
    bias = pl.pallas_call(
        functools.partial(_dsa_pick_kernel, topk=topk, rows=rows, nkeys=nkeys),
        grid=(b // rows,),
        in_specs=[pl.BlockSpec((rows, nkeys), lambda i: (i, 0))],
        out_specs=pl.BlockSpec((rows, nkeys), lambda i: (i, 0)),
        out_shape=jax.ShapeDtypeStruct((b, nkeys), F32),
        compiler_params=_cparams("parallel"),
        name="dsa_sample_pick",
    )(scores.reshape(b, nkeys)).reshape(b, n_pages + 1, PAGE)

    hpg = N_HEADS // N_KV
    qh = q.reshape(b, N_HEADS, HEAD_DIM)
    qb = jnp.concatenate(
        [jnp.pad(qh[:, g * hpg:(g + 1) * hpg], ((0, 0), (0, 0), (g * HEAD_DIM, KV_W - (g + 1) * HEAD_DIM)))
         for g in range(N_KV)], axis=1)
    o8 = pl.pallas_call(
        functools.partial(_dsa_att_kernel, npg=npg, n_pages=n_pages),
        grid_spec=pltpu.PrefetchScalarGridSpec(
            num_scalar_prefetch=1,
            grid=(b, n_pages // npg),
            in_specs=[per_seq(8, KV_W), per_seq(n_pages + 1, 128), per_seq(1, KV_W), per_seq(1, KV_W)]
                     + [page(KV_W, i) for i in range(npg)] * 2,
            out_specs=per_seq(8, KV_W),
            scratch_shapes=[pltpu.VMEM((8, 1), F32), pltpu.VMEM((8, 1), F32), pltpu.VMEM((8, KV_W), F32)]),
        out_shape=jax.ShapeDtypeStruct((b, 8, KV_W), F32),
        compiler_params=_cparams("parallel", "arbitrary"),
        name="dsa_sample_attend",
    )(pt, qb, bias, k_new[:, None, :], v_new[:, None, :], *([ck] * npg), *([cv] * npg))
    o = jnp.concatenate([o8[:, h, (h // hpg) * HEAD_DIM:(h // hpg + 1) * HEAD_DIM] for h in range(N_HEADS)], axis=-1)
    return o.astype(MXU_DTYPE)


def _gdn_kernel(cin_ref, gz_ref, gab_ref, cbuf_ref, s0_ref, cw_ref, alog_ref, dtb_ref, gng_ref,
                o_ref, cnew_ref, sfin_ref, tail_ref, s_ref, *, tb, tbv, c, nblk):
    j = pl.program_id(1)
    nc = tb // c

    @pl.when(j == 0)
    def _():
        tail_ref[...] = cbuf_ref[...]
        s_ref[...] = s0_ref[...]

    full = jnp.concatenate([tail_ref[...], cin_ref[...]], axis=0)
    first = 8 - (CONV_W - 1)
    y = full[first:first + tb] * cw_ref[0:1, :]
    for jj in range(1, CONV_W):
        y = y + full[first + jj:first + jj + tb] * cw_ref[jj:jj + 1, :]
    y = _silu(y)
    tail_ref[...] = full[tbv:tbv + 8]

    def per_head(x, width):
        return jnp.concatenate([x[:, width * h:width * (h + 1)].reshape(nc, c, width) for h in range(G_HEADS)], axis=0)

    q = per_head(y[:, :G_KW], G_DK)
    k = per_head(y[:, G_KW:2 * G_KW], G_DK)
    v = per_head(y[:, 2 * G_KW:], G_DV)
    q = q * lax.rsqrt(jnp.sum(q * q, axis=-1, keepdims=True) + EPS) * G_DK ** -0.5
    k = k * lax.rsqrt(jnp.sum(k * k, axis=-1, keepdims=True) + EPS)

    gab = gab_ref[...]
    xa = gab + dtb_ref[...]
    g_t = -jnp.exp(alog_ref[...]) * (jnp.maximum(xa, 0.0) + jnp.log(1.0 + jnp.exp(-jnp.abs(xa))))
    beta_t = 1.0 / (1.0 + jnp.exp(-gab))
    if tbv < tb:
        live = _iota2((tb, 128), 0) < tbv
        g_t = jnp.where(live, g_t, 0.0)
        beta_t = jnp.where(live, beta_t, 0.0)
    lower = _tri(c, "lower")
    ones_lower = jnp.broadcast_to(_mx(jnp.where(lower, 1.0, 0.0)), (nc, c, c))
    ones_upper = jnp.broadcast_to(_mx(jnp.where(_tri(c, "upper"), 1.0, 0.0)), (nc, c, c))
    parts = _split3(g_t.reshape(nc, c, 128))
    gc3 = sum(_bdot(ones_lower, p, 2, 1) for p in parts)
    gcr3 = sum(_bdot(p, ones_upper, 1, 1) for p in parts)
    beta3 = beta_t.reshape(nc, c, 128)
    gc = jnp.concatenate([gc3[:, :, h:h + 1] for h in range(G_HEADS)], axis=0)
    beta = jnp.concatenate([beta3[:, :, G_HEADS + h:G_HEADS + h + 1] for h in range(G_HEADS)], axis=0)
    gc_row = jnp.concatenate([gcr3[:, h:h + 1, :] for h in range(G_HEADS)], axis=0)
    g_last = gc[:, c - 1:c, :]
    decay = jnp.where(lower, jnp.exp(jnp.where(lower, gc - gc_row, 0.0)), 0.0)
    kb = k * beta
    vb = v * beta
    eg = jnp.exp(gc)
    kk = _bdot(jnp.concatenate([kb, q], axis=1), k, 2, 2)
    a_mat = jnp.where(_tri(c, "strict_lower"), kk[:, :c] * decay, 0.0)
    qk = jnp.where(lower, kk[:, c:] * decay, 0.0)
    tinv = jnp.where(_tri(c, "eye"), 1.0, 0.0) - a_mat
    apow = a_mat
    width = 2
    while width < c:
        apow2 = _bdot(apow, apow, 2, 1)
        tinv = tinv + _bdot(tinv, apow2, 2, 1)
        apow = apow2
        width *= 2
    uw = _bdot(tinv, jnp.concatenate([vb, kb * eg], axis=2), 2, 1)
    u = uw[:, :, :G_DV]
    wq = _mx(jnp.concatenate([uw[:, :, G_DV:], q * eg], axis=1))
    kd = _mx(k * jnp.exp(g_last - gc))
    eg_last = jnp.exp(g_last)
    qk = _mx(qk)

    state = s_ref[...]
    outs = []
    for ci in range(nc):
        pick = lambda a: jnp.concatenate([a[h * nc + ci:h * nc + ci + 1] for h in range(G_HEADS)], axis=0)
        ws = _bdot(pick(wq), state, 2, 1)
        v_new = pick(u) - ws[:, :c]
        outs.append(ws[:, c:] + _bdot(pick(qk), v_new, 2, 1))
        state = state * pick(eg_last) + _bdot(pick(kd), v_new, 1, 1)
    s_ref[...] = state

    z = gz_ref[...]
    for h in range(G_HEADS):
        o = jnp.concatenate([outs[ci][h] for ci in range(nc)], axis=0)
        o = o * lax.rsqrt(jnp.mean(o * o, axis=-1, keepdims=True) + EPS) * gng_ref[...]
        o_ref[:, G_DV * h:G_DV * (h + 1)] = (o * _silu(z[:, G_DV * h:G_DV * (h + 1)])).astype(o_ref.dtype)

    @pl.when(j == nblk - 1)
    def _():
        cnew_ref[...] = tail_ref[...]
        sfin_ref[...] = s_ref[...]


def _gdn(cin, gz, gab, cbuf8, s0, conv_w, alog, dtb, gng, t_valid):
    b, t, _ = cin.shape
    if t_valid == t:
        c = min(GDN_CHUNK, t)
        tb = min(ROW_TILE, t)
        tbv = tb
    else:
        c = tb = t
        tbv = t_valid
    nblk = t // tb
    blk = lambda w: pl.BlockSpec((None, tb, w), lambda bb, j: (bb, j, 0))
    per_b = lambda *shape: pl.BlockSpec((None,) + shape, lambda bb, j: (bb,) + (0,) * len(shape))
    full = lambda a: pl.BlockSpec(a.shape, lambda bb, j: (0,) * a.ndim)
    return pl.pallas_call(
        functools.partial(_gdn_kernel, tb=tb, tbv=tbv, c=c, nblk=nblk),
        grid=(b, nblk),
        in_specs=[blk(CONV_DIM), blk(G_VW), blk(128), per_b(8, CONV_DIM), per_b(G_HEADS, G_DK, G_DV),
                  full(conv_w), full(alog), full(dtb), full(gng)],
        out_specs=[blk(G_VW), per_b(8, CONV_DIM), per_b(G_HEADS, G_DK, G_DV)],
        out_shape=[jax.ShapeDtypeStruct((b, t, G_VW), MXU_DTYPE), jax.ShapeDtypeStruct((b, 8, CONV_DIM), F32),
                   jax.ShapeDtypeStruct((b, G_HEADS, G_DK, G_DV), F32)],
        scratch_shapes=[pltpu.VMEM((8, CONV_DIM), F32), pltpu.VMEM((G_HEADS, G_DK, G_DV), F32)],
        compiler_params=_cparams("parallel", "arbitrary"),
        name="gdn",
    )(cin, gz, gab, cbuf8, s0, conv_w, alog, dtb, gng)


def _mem_kv_kernel(x_ref, g_ref, w_ref, kg_ref, mkt_ref, mvt_ref):
    x = x_ref[...]
    h = _mx(x * lax.rsqrt(jnp.mean(x * x, axis=-1, keepdims=True) + EPS) * g_ref[...])
    p = jnp.dot(h, w_ref[...], preferred_element_type=F32)
    mk = p[:, :MEM_W]
    mk = mk * lax.rsqrt(_dot(mk * mk, _group_mean_matrix(MEM_W, MEM_HD)) + EPS) * kg_ref[...]
    mkt_ref[...] = mk.T
    mvt_ref[...] = p[:, MEM_W:].T


def _mem_kv(mem, g, w_kv, kg):
    b, m, _ = mem.shape
    full = lambda a: pl.BlockSpec(a.shape, lambda i: (0,) * a.ndim)
    return pl.pallas_call(
        _mem_kv_kernel,
        grid=(b,),
        in_specs=[pl.BlockSpec((None, m, D_MODEL), lambda i: (i, 0, 0)), full(g), full(w_kv), full(kg)],
        out_specs=[pl.BlockSpec((None, MEM_W, m), lambda i: (i, 0, 0))] * 2,
        out_shape=[jax.ShapeDtypeStruct((b, MEM_W, m), F32)] * 2,
        compiler_params=_cparams("parallel"),
        name="mem_kv",
    )(mem, g, w_kv, kg)


def _mix_out_kernel(x_ref, oa_ref, og_ref, woa_ref, wog_ref, g_ref, wq_ref, qg_ref, x1_ref, qm_ref):
    x1 = (x_ref[...] + jnp.dot(oa_ref[...], woa_ref[...], preferred_element_type=F32)
          + jnp.dot(og_ref[...], wog_ref[...], preferred_element_type=F32))
    x1_ref[...] = x1
    h = _mx(x1 * lax.rsqrt(jnp.mean(x1 * x1, axis=-1, keepdims=True) + EPS) * g_ref[...])
    q = jnp.dot(h, wq_ref[...], preferred_element_type=F32)
    q = q * lax.rsqrt(_dot(q * q, _group_mean_matrix(MEM_W, MEM_HD)) + EPS) * qg_ref[...]
    qm_ref[...] = (q * MEM_HD ** -0.5).astype(qm_ref.dtype)


def _mix_out(x2d, oa, og, w_oa, w_og, g, w_mq, qg, tm):
    n = x2d.shape[0]
    row = lambda w: pl.BlockSpec((tm, w), lambda i: (i, 0))
    full = lambda a: pl.BlockSpec(a.shape, lambda i: (0,) * a.ndim)
    return pl.pallas_call(
        _mix_out_kernel,
        grid=(n // tm,),
        in_specs=[row(D_MODEL), row(ATTN_W), row(G_VW), full(w_oa), full(w_og), full(g), full(w_mq), full(qg)],
        out_specs=[row(D_MODEL), row(MEM_W)],
        out_shape=[jax.ShapeDtypeStruct((n, D_MODEL), F32), jax.ShapeDtypeStruct((n, MEM_W), MXU_DTYPE)],
        compiler_params=_cparams("parallel"),
        name="mix_out",
    )(x2d, oa, og, w_oa, w_og, g, w_mq, qg)


def _mem_attn_kernel(q_ref, mkt_ref, mvt_ref, o_ref):
    for h in range(MEM_HEADS):
        cols = slice(MEM_HD * h, MEM_HD * (h + 1))
        s = _dot(q_ref[:, cols], mkt_ref[cols, :])
        p = jnp.exp(s - jnp.max(s, axis=1, keepdims=True))
        o = _dot_nt(p, mvt_ref[cols, :]) / jnp.sum(p, axis=1, keepdims=True)
        o_ref[:, cols] = o.astype(o_ref.dtype)


def _mem_attn(qm, mkt, mvt, tm):
    b, t, _ = qm.shape
    m = mkt.shape[2]
    return pl.pallas_call(
        _mem_attn_kernel,
        grid=(b, t // tm),
        in_specs=[pl.BlockSpec((None, tm, MEM_W), lambda bb, i: (bb, i, 0)),
                  pl.BlockSpec((None, MEM_W, m), lambda bb, i: (bb, 0, 0)),
                  pl.BlockSpec((None, MEM_W, m), lambda bb, i: (bb, 0, 0))],
        out_specs=pl.BlockSpec((None, tm, MEM_W), lambda bb, i: (bb, i, 0)),
        out_shape=jax.ShapeDtypeStruct((b, t, MEM_W), MXU_DTYPE),
        compiler_params=_cparams("parallel", "arbitrary"),
        name="mem_attn",
    )(qm, mkt, mvt)


def _ffn_kernel(x1_ref, om_ref, wmo_ref, g_ref, wg_ref, wu_ref, wd_ref, y_ref, act_ref, *, fc):
    x2 = x1_ref[...] + jnp.dot(om_ref[...], wmo_ref[...], preferred_element_type=F32)
    h = _mx(x2 * lax.rsqrt(jnp.mean(x2 * x2, axis=-1, keepdims=True) + EPS) * g_ref[...])
    d_ff = wg_ref.shape[1]
    for f0 in range(0, d_ff, fc):
        gate = jnp.dot(h, wg_ref[:, f0:f0 + fc], preferred_element_type=F32)
        up = jnp.dot(h, wu_ref[:, f0:f0 + fc], preferred_element_type=F32)
        act_ref[:, f0:f0 + fc] = (_silu(gate) * up).astype(act_ref.dtype)
    y_ref[...] = x2 + jnp.dot(act_ref[...], wd_ref[...], preferred_element_type=F32)


def _ffn(x1, om, w_mo, g, w_gate, w_up, w_down, tm):
    n = x1.shape[0]
    d_ff = w_gate.shape[1]
    row = lambda w: pl.BlockSpec((tm, w), lambda i: (i, 0))
    full = lambda a: pl.BlockSpec(a.shape, lambda i: (0,) * a.ndim, pipeline_mode=pl.Buffered(1))
    return pl.pallas_call(
        functools.partial(_ffn_kernel, fc=256),
        grid=(n // tm,),
        in_specs=[row(D_MODEL), row(MEM_W), full(w_mo), full(g), full(w_gate), full(w_up), full(w_down)],
        out_specs=row(D_MODEL),
        out_shape=jax.ShapeDtypeStruct((n, D_MODEL), F32),
        scratch_shapes=[pltpu.VMEM((tm, d_ff), MXU_DTYPE)],
        compiler_params=_cparams("parallel"),
        name="ffn",
    )(x1, om, w_mo, g, w_gate, w_up, w_down)


def _prep_weights(attn_norm_g, w_in, q_norm_g, k_norm_g, conv_w, a_log, dt_bias, gdn_norm_g, w_out,
                  xattn_norm_g, w_mq, mq_norm_g, w_mo, ffn_norm_g, w_gate, w_up, w_down):
    lane_pad = lambda a: jnp.pad(a, (0, 128 - a.shape[0]))[None, :]
    return dict(
        attn_g=attn_norm_g[None, :],
        w_dsa=_mx(jnp.pad(w_in[:, :DSA_COLS], ((0, 0), (0, DSA_PAD - DSA_COLS)))),
        w_gdn=_mx(jnp.pad(w_in[:, DSA_COLS:], ((0, 0), (0, GDN_PAD - GDN_COLS)))),
        qg=jnp.tile(q_norm_g, N_HEADS)[None, :],
        kg=jnp.tile(k_norm_g, N_KV)[None, :],
        conv_w=conv_w, alog=lane_pad(a_log), dtb=lane_pad(dt_bias), gng=gdn_norm_g[None, :],
        w_oa=_mx(w_out[:ATTN_W]), w_og=_mx(w_out[ATTN_W:]),
        xattn_g=xattn_norm_g[None, :], w_mq=_mx(w_mq), mqg=jnp.tile(mq_norm_g, MEM_HEADS)[None, :],
        w_mo=_mx(w_mo), ffn_g=ffn_norm_g[None, :], w_gate=_mx(w_gate), w_up=_mx(w_up), w_down=_mx(w_down))


def _trunk_layer(x, mem_kt, mem_vt, conv_buf, ssm0, dsa_fn, w, t_pad):
    b, t, _ = x.shape
    n = b * t
    tm = min(ROW_TILE, n)
    x2d = x.reshape(n, D_MODEL)
    q, k, v, kiwi, kvi, qi, cin, gz, gab = _proj_in(x2d, w["attn_g"], w["w_dsa"], w["w_gdn"], w["qg"], w["kg"], tm)
    o_attn = dsa_fn(q, qi, kiwi, kvi, k, v)
    pad_t = lambda a: jnp.pad(a.reshape(b, t, -1), ((0, 0), (0, t_pad - t), (0, 0)))
    cbuf8 = jnp.pad(conv_buf, ((0, 0), (8 - (CONV_W - 1), 0), (0, 0)))
    o_gdn, cnew8, ssm_new = _gdn(pad_t(cin), pad_t(gz), pad_t(gab), cbuf8, ssm0,
                                 w["conv_w"], w["alog"], w["dtb"], w["gng"], t)
    o_gdn = o_gdn[:, :t].reshape(n, G_VW)
    x1, qm = _mix_out(x2d, o_attn, o_gdn, w["w_oa"], w["w_og"], w["xattn_g"], w["w_mq"], w["mqg"], tm)
    om = _mem_attn(pad_t(qm), mem_kt, mem_vt, min(ROW_TILE, t_pad))[:, :t].reshape(n, MEM_W)
    y = _ffn(x1, om, w["w_mo"], w["ffn_g"], w["w_gate"], w["w_up"], w["w_down"], tm)
    new = (k.reshape(b, t, N_KV, HEAD_DIM), v.reshape(b, t, N_KV, HEAD_DIM), kiwi[:, :IDX_DIM].reshape(b, t, IDX_DIM),
           cnew8[:, 8 - (CONV_W - 1):], ssm_new)
    return y.reshape(b, t, D_MODEL), new


def kernel(x_prompt, x_sample, mem_prompt, cache_k, cache_v, cache_kidx, page_table, state_conv, state_ssm,
           cache_mem_k, cache_mem_v, attn_norm_g, w_in, q_norm_g, k_norm_g, conv_w, a_log, dt_bias, gdn_norm_g,
           w_out, xattn_norm_g, mem_norm_g, w_mq, w_mk, w_mv, mq_norm_g, mk_norm_g, w_mo, ffn_norm_g, w_gate,
           w_up, w_down):
    layer_w = (attn_norm_g, w_in, q_norm_g, k_norm_g, conv_w, a_log, dt_bias, gdn_norm_g, w_out,
               xattn_norm_g, w_mq, mq_norm_g, w_mo, ffn_norm_g, w_gate, w_up, w_down)
    depth = w_in.shape[0]
    bp, tp, _ = x_prompt.shape
    bs, ts, _ = x_sample.shape
    assert ts == 1, "the sample group decodes one token per sequence"
    m = mem_prompt.shape[1]
    feature_major = lambda a: jnp.transpose(a, (0, 2, 3, 1)).reshape(a.shape[0], MEM_W, m)
    token_major = lambda a: jnp.transpose(a.reshape(a.shape[0], MEM_HEADS, MEM_HD, m), (0, 3, 1, 2))
    xp, xs = x_prompt, x_sample
    new_p, new_s = [], []
    for l in range(depth):
        w = _prep_weights(*[a[l] for a in layer_w])
        mkt_p, mvt_p = _mem_kv(mem_prompt, mem_norm_g[l][None, :], _mx(jnp.concatenate([w_mk[l], w_mv[l]], axis=1)),
                               jnp.tile(mk_norm_g[l], MEM_HEADS)[None, :])
        dsa_p = lambda q, qi, kiwi, kvi, k, v: _dsa_prompt(q, qi, kiwi, kvi, bp, tp)
        xp, st_p = _trunk_layer(xp, mkt_p, mvt_p, jnp.zeros((bp, CONV_W - 1, CONV_DIM), F32),
                                jnp.zeros((bp, G_HEADS, G_DK, G_DV), F32), dsa_p, w, tp)
        dsa_s = lambda q, qi, kiwi, kvi, k, v: _dsa_sample(q, qi, kiwi, k, v, cache_k[l], cache_v[l],
                                                           cache_kidx[l], page_table)
        xs, st_s = _trunk_layer(xs, feature_major(cache_mem_k[l]), feature_major(cache_mem_v[l]),
                                state_conv[l], state_ssm[l], dsa_s, w, 8)
        new_p.append(st_p + (token_major(mkt_p), token_major(mvt_p)))
        new_s.append(st_s)
    k_p, v_p, kidx_p, conv_p, ssm_p, memk_p, memv_p = [jnp.stack(z) for z in zip(*new_p)]
    k_s, v_s, kidx_s, conv_s, ssm_s = [jnp.stack(z) for z in zip(*new_s)]
    return (xp, xs, k_p, v_p, kidx_p, conv_p, ssm_p, memk_p, memv_p, k_s, v_s, kidx_s, conv_s, ssm_s)
```

```python
import functools

import jax
import jax.numpy as jnp
from jax import lax
from jax.experimental import pallas as pl
from jax.experimental.pallas import tpu as pltpu

F32 = jnp.float32
MXU_DTYPE = jnp.bfloat16

D_MODEL = 1024
PAGE = 128
N_HEADS, N_KV, HEAD_DIM = 8, 2, 64
ATTN_W, KV_W = N_HEADS * HEAD_DIM, N_KV * HEAD_DIM
IDX_HEADS, IDX_DIM = 4, 64
TOPK_MAX = 256
G_HEADS, G_DK, G_DV = 4, 128, 128
G_KW, G_VW = G_HEADS * G_DK, G_HEADS * G_DV
CONV_W = 4
CONV_DIM = 2 * G_KW + G_VW
GDN_CHUNK = 64
MEM_HEADS, MEM_HD = 4, 64
MEM_W = MEM_HEADS * MEM_HD
DSA_COLS = ATTN_W + 2 * KV_W + IDX_HEADS * IDX_DIM + IDX_DIM + IDX_HEADS
DSA_PAD = 1152
GDN_COLS = 2 * G_KW + 2 * G_VW + 2 * G_HEADS
GDN_PAD = 2176
EPS = 1e-6
NEG = -1e30
INT_MIN = -2 ** 31
KEY_NEG_INF = -2139095041
FLT_MAX = 3.4028234663852886e38
VMEM_LIMIT = 56 * 1024 * 1024
ROW_TILE = 512
DSA_Q_BLOCK = 256
DSA_KEY_CHUNK = 256
BISECT_STEPS_PER_CHECK = 4
SAMPLE_PAGES_PER_STEP = 16
LOG2E = 1.4426950408889634


def _cparams(*sem):
    return pltpu.CompilerParams(dimension_semantics=sem, vmem_limit_bytes=VMEM_LIMIT)


def _mx(a):
    return a.astype(MXU_DTYPE)


def _dot(a, b):
    return jnp.dot(_mx(a), _mx(b), preferred_element_type=F32)


def _dot_nt(a, b):
    return lax.dot_general(_mx(a), _mx(b), (((1,), (1,)), ((), ())), preferred_element_type=F32)


def _bdot(a, b, ca, cb):
    return lax.dot_general(_mx(a), _mx(b), (((ca,), (cb,)), ((0,), (0,))), preferred_element_type=F32)


def _split3(a):
    hi = _mx(a)
    r1 = a - hi.astype(F32)
    mid = _mx(r1)
    lo = _mx(r1 - mid.astype(F32))
    return hi, mid, lo


def _iota2(shape, dim):
    return lax.broadcasted_iota(jnp.int32, shape, dim)


def _tri(n, kind):
    r, c = _iota2((n, n), 0), _iota2((n, n), 1)
    return {"upper": r <= c, "lower": r >= c, "strict_lower": r > c, "eye": r == c}[kind]


def _group_mean_matrix(n, group):
    shift = group.bit_length() - 1
    r, c = _iota2((n, n), 0) >> shift, _iota2((n, n), 1) >> shift
    return jnp.where(r == c, 1.0 / group, 0.0).astype(MXU_DTYPE)


def _silu(x):
    return x * (1.0 / (1.0 + jnp.exp(-x)))


def _key_to_float(key):
    bits = jnp.where(key < 0, key ^ jnp.int32(0x7FFFFFFF), key)
    return jnp.where(key < KEY_NEG_INF, -jnp.inf, lax.bitcast_convert_type(bits, F32))


def _next_candidate(it, base):
    return jnp.where(it == 0, jnp.zeros_like(base), base | lax.shift_left(jnp.int32(1), 31 - it))


def _proj_in_kernel(x_ref, g_ref, wd_ref, wg_ref, qg_ref, kg_ref, *outs, feature_major):
    if feature_major:
        q_ref, kiwi_ref, kt_ref, vt_ref, kit_ref, ktb_ref, va_ref, qi_ref, cin_ref, gz_ref, gab_ref = outs
    else:
        q_ref, kiwi_ref, k_ref, v_ref, qi_ref, cin_ref, gz_ref, gab_ref = outs
    x = x_ref[...]
    h = _mx(x * lax.rsqrt(jnp.mean(x * x, axis=-1, keepdims=True) + EPS) * g_ref[...])
    pd = jnp.dot(h, wd_ref[...], preferred_element_type=F32)
    q = pd[:, :ATTN_W]
    k = pd[:, ATTN_W:ATTN_W + KV_W]
    v = pd[:, ATTN_W + KV_W:ATTN_W + 2 * KV_W]
    qi = pd[:, ATTN_W + 2 * KV_W:ATTN_W + 2 * KV_W + IDX_HEADS * IDX_DIM]
    kiwi = pd[:, 1024:1152]
    m64 = _group_mean_matrix(ATTN_W, HEAD_DIM)
    q = q * lax.rsqrt(_dot(q * q, m64) + EPS) * qg_ref[...]
    k = k * lax.rsqrt(_dot(k * k, m64[:KV_W, :KV_W]) + EPS) * kg_ref[...]
    q_ref[...] = (q * (HEAD_DIM ** -0.5 * LOG2E)).astype(q_ref.dtype)
    kiwi_ref[...] = kiwi
    qi_ref[...] = (qi * IDX_DIM ** -0.5).astype(qi_ref.dtype)
    if feature_major:
        kt, vt, kit = k.T, v.T, kiwi.T[:IDX_DIM]
        kt_ref[...] = kt
        vt_ref[...] = vt
        kit_ref[...] = kit
        ktb_ref[0:KV_W, :] = kt.astype(ktb_ref.dtype)
        ktb_ref[KV_W:KV_W + IDX_DIM, :] = kit.astype(ktb_ref.dtype)
        lane = _iota2(v.shape, 1)
        for g, vg in enumerate((v, jnp.concatenate([v[:, HEAD_DIM:], v[:, :HEAD_DIM]], axis=1))):
            aug = jnp.where(lane < HEAD_DIM, vg, jnp.where(lane == HEAD_DIM, 1.0, 0.0))
            va_ref[:, 128 * g:128 * (g + 1)] = aug.astype(va_ref.dtype)
    else:
        k_ref[...] = k
        v_ref[...] = v
    pg = jnp.dot(h, wg_ref[...], preferred_element_type=F32)
    cin_ref[...] = pg[:, :CONV_DIM]
    gz_ref[...] = pg[:, CONV_DIM:CONV_DIM + G_VW]
    gab_ref[...] = pg[:, CONV_DIM + G_VW:GDN_PAD]


def _proj_in(x, g, w_dsa, w_gdn, qg, kg, tm, feature_major):
    b, t, _ = x.shape
    nt = t // tm
    row = lambda w: pl.BlockSpec((tm, w), lambda bb, i: (bb * nt + i, 0))
    col = lambda w: pl.BlockSpec((None, w, tm), lambda bb, i: (bb, 0, i))
    full = lambda a: pl.BlockSpec(a.shape, lambda bb, i: (0,) * a.ndim)
    rows = lambda w, dt: (row(w), jax.ShapeDtypeStruct((b * t, w), dt))
    cols = lambda w, dt: (col(w), jax.ShapeDtypeStruct((b, w, t), dt))
    outs = [rows(ATTN_W, MXU_DTYPE), rows(128, F32)]
    if feature_major:
        outs += [cols(KV_W, F32), cols(KV_W, F32), cols(IDX_DIM, F32), cols(KV_W + IDX_DIM, MXU_DTYPE),
                 (pl.BlockSpec((None, tm, 256), lambda bb, i: (bb, i, 0)), jax.ShapeDtypeStruct((b, t, 256), MXU_DTYPE))]
    else:
        outs += [rows(KV_W, F32), rows(KV_W, F32)]
    outs += [rows(IDX_HEADS * IDX_DIM, MXU_DTYPE), rows(CONV_DIM, F32), rows(G_VW, F32), rows(128, F32)]
    return pl.pallas_call(
        functools.partial(_proj_in_kernel, feature_major=feature_major),
        grid=(b, nt),
        in_specs=[pl.BlockSpec((None, tm, D_MODEL), lambda bb, i: (bb, i, 0)),
                  full(g), full(w_dsa), full(w_gdn), full(qg), full(kg)],
        out_specs=[o[0] for o in outs],
        out_shape=[o[1] for o in outs],
        compiler_params=_cparams("parallel", "parallel"),
        name="proj_in",
    )(x, g, w_dsa, w_gdn, qg, kg)


def _dsa_prompt_kernel(q_ref, qi_ref, kiwi_ref, kt_ref, va_ref, o_ref, s_ref, bias_ref, *, topk, tq, nkeys, row0):
    kit = kt_ref[KV_W:KV_W + IDX_DIM, :]
    wi = kiwi_ref[:, IDX_DIM:IDX_DIM + IDX_HEADS] * IDX_HEADS ** -0.5
    score = None
    for h in range(IDX_HEADS):
        rel = jnp.maximum(_dot(qi_ref[:, IDX_DIM * h:IDX_DIM * (h + 1)], kit), 0.0)
        term = wi[:, h:h + 1] * rel
        score = term if score is None else score + term
    adm = _iota2((tq, nkeys), 1) <= _iota2((tq, nkeys), 0) + row0
    s_ref[...] = jnp.where(adm, score, -jnp.inf)
    _topk_bias(s_ref, bias_ref, topk, tq, nkeys)
    kc = min(DSA_KEY_CHUNK, nkeys)
    for g in range(N_KV):
        for r in range(N_HEADS // N_KV):
            h = g * (N_HEADS // N_KV) + r
            qh = q_ref[:, HEAD_DIM * h:HEAD_DIM * (h + 1)]
            mm = jnp.full((tq, 128), NEG, F32)
            for c0 in range(0, nkeys, kc):
                ks = slice(c0, c0 + kc)
                sc = _dot(qh, kt_ref[HEAD_DIM * g:HEAD_DIM * (g + 1), ks]) + bias_ref[:, ks]
                s_ref[:, ks] = sc
                for l0 in range(0, kc, 128):
                    mm = jnp.maximum(mm, sc[:, l0:l0 + 128])
            m = jnp.max(mm, axis=1, keepdims=True)
            acc = jnp.zeros((tq, 128), F32)
            for c0 in range(0, nkeys, kc):
                ks = slice(c0, c0 + kc)
                p = jnp.exp2(_mx(s_ref[:, ks] - m))
                acc = acc + jnp.dot(p, va_ref[ks, 128 * g:128 * (g + 1)], preferred_element_type=F32)
            o_ref[:, HEAD_DIM * h:HEAD_DIM * (h + 1)] = (
                acc[:, :HEAD_DIM] / acc[:, HEAD_DIM:HEAD_DIM + 1]).astype(o_ref.dtype)


def _topk_bias(s_ref, bias_ref, topk, tq, nkeys):
    halves = 2
    hr = tq // halves
    rows = [slice(p * hr, (p + 1) * hr) for p in range(halves)]
    n_adm = [jnp.sum(jnp.where(s_ref[r, :] > -jnp.inf, 1.0, 0.0), axis=1, keepdims=True) for r in rows]

    def pending(cnts):
        worst = [jnp.max(jnp.where(n > float(topk), jnp.abs(c - float(topk)), 0.0)) for n, c in zip(n_adm, cnts)]
        return functools.reduce(jnp.maximum, worst) > 0.0

    def body(state):
        it, _, bases, cnts = state
        bases, cnts = list(bases), list(cnts)
        for j in range(BISECT_STEPS_PER_CHECK):
            for p, r in enumerate(rows):
                cand = _next_candidate(it + j, bases[p])
                cnt = jnp.sum(jnp.where(s_ref[r, :] >= _key_to_float(cand), 1.0, 0.0), axis=1, keepdims=True)
                ok = cnt >= float(topk)
                bases[p] = jnp.where(ok, cand, bases[p])
                cnts[p] = jnp.where(ok, cnt, cnts[p])
        return it + BISECT_STEPS_PER_CHECK, pending(cnts), tuple(bases), tuple(cnts)

    cnts0 = tuple(jnp.full((hr, 1), float(nkeys), F32) for _ in rows)
    bases0 = tuple(jnp.full((hr, 1), INT_MIN, jnp.int32) for _ in rows)
    _, _, bases, _ = lax.while_loop(lambda st: (st[0] < 32) & st[1], body,
                                    (jnp.int32(0), pending(cnts0), bases0, cnts0))
    thr = jnp.maximum(_key_to_float(jnp.concatenate(bases, axis=0)), -FLT_MAX)
    thr = jnp.where(jnp.concatenate(n_adm, axis=0) <= float(topk), -FLT_MAX, thr)
    s = s_ref[...]
    gt = s > thr
    tie = s == thr
    need = float(topk) - jnp.sum(jnp.where(gt, 1.0, 0.0), axis=1, keepdims=True)
    n_tie = jnp.sum(jnp.where(tie, 1.0, 0.0), axis=1, keepdims=True)
    crowded = jnp.max(n_tie - need) > 0.0

    @pl.when(jnp.logical_not(crowded))
    def _():
        bias_ref[...] = jnp.where(s >= thr, 0.0, NEG)

    @pl.when(crowded)
    def _():
        tie_f = _mx(jnp.where(tie, 1.0, 0.0))
        upper = _mx(jnp.where(_tri(128, "upper"), 1.0, 0.0))
        off = jnp.zeros((tq, 1), F32)
        parts = []
        for c in range(nkeys // 128):
            pc = jnp.dot(tie_f[:, 128 * c:128 * (c + 1)], upper, preferred_element_type=F32) + off
            off = pc[:, 127:128]
            parts.append(pc)
        rank = jnp.concatenate(parts, axis=1)
        bias_ref[...] = jnp.where(gt | (tie & (rank <= need)), 0.0, NEG)


def _dsa_prompt(q, qi, kiwi, ktb, va, b, t):
    tq = min(DSA_Q_BLOCK, t)
    topk = min(TOPK_MAX, t // 4)
    nq = t // tq
    outs = []
    for i in range(nq):
        nkeys = (i + 1) * tq
        blk = lambda w, i=i: pl.BlockSpec((tq, w), lambda bb: (bb * nq + i, 0))
        outs.append(pl.pallas_call(
            functools.partial(_dsa_prompt_kernel, topk=topk, tq=tq, nkeys=nkeys, row0=i * tq),
            grid=(b,),
            in_specs=[blk(ATTN_W), blk(IDX_HEADS * IDX_DIM), blk(128),
                      pl.BlockSpec((None, KV_W + IDX_DIM, nkeys), lambda bb: (bb, 0, 0)),
                      pl.BlockSpec((None, nkeys, 256), lambda bb: (bb, 0, 0))],
            out_specs=pl.BlockSpec((None, tq, ATTN_W), lambda bb: (bb, 0, 0)),
            out_shape=jax.ShapeDtypeStruct((b, tq, ATTN_W), MXU_DTYPE),
            scratch_shapes=[pltpu.VMEM((tq, nkeys), F32), pltpu.VMEM((tq, nkeys), F32)],
            compiler_params=_cparams("parallel"),
            name=f"dsa_prompt_{i}",
        )(q, qi, kiwi, ktb, va))
    return jnp.concatenate(outs, axis=1).reshape(b * t, ATTN_W)


def _dsa_score_kernel(pt_ref, qi_ref, wib_ref, kis_ref, *rest, npg, n_pages):
    pages = rest[:npg]
    sc_ref = rest[npg]
    j = pl.program_id(1)
    qi = qi_ref[...]
    w = wib_ref[...][:, 0:1]
    z = jnp.concatenate([_dot(qi, pages[i][...]) for i in range(npg)], axis=1)
    sc = jnp.sum(w * jnp.maximum(z, 0.0), axis=0, keepdims=True)
    for i in range(npg):
        sc_ref[pl.ds(j * npg + i, 1), :] = sc[:, PAGE * i:PAGE * (i + 1)]

    @pl.when(j == n_pages // npg - 1)
    def _():
        z_self = jnp.sum(qi.astype(F32) * _mx(kis_ref[...]).astype(F32), axis=1, keepdims=True)
        s_self = jnp.sum(w * jnp.maximum(z_self, 0.0), axis=0, keepdims=True)
        sc_ref[n_pages:n_pages + 1, :] = jnp.where(_iota2((1, 128), 1) == 0, s_self, -jnp.inf)


def _dsa_pick_kernel(s_ref, bias_ref, *, topk, rows, nkeys):
    _topk_bias(s_ref, bias_ref, topk, rows, nkeys)


def _dsa_att_kernel(pt_ref, qb_ref, bias_ref, kself_ref, vself_ref, *rest, npg, n_pages):
    kp = rest[:npg]
    vp = rest[npg:2 * npg]
    o_ref, m_ref, l_ref, acc_ref = rest[2 * npg:]
    j = pl.program_id(1)

    @pl.when(j == 0)
    def _():
        m_ref[...] = jnp.full(m_ref.shape, NEG, F32)
        l_ref[...] = jnp.zeros(l_ref.shape, F32)
        acc_ref[...] = jnp.zeros(acc_ref.shape, F32)

    qb = qb_ref[...]
    valid = jnp.concatenate([bias_ref[pl.ds(j * npg + i, 1), :] for i in range(npg)], axis=1) > -1.0
    s = jnp.concatenate([_dot(qb, kp[i][...]) for i in range(npg)], axis=1)
    s = jnp.where(valid, s, NEG)
    m_old = m_ref[...]
    m_new = jnp.maximum(m_old, jnp.max(s, axis=1, keepdims=True))
    alpha = jnp.exp2(m_old - m_new)
    p = jnp.where(valid, jnp.exp2(s - m_new), 0.0)
    pv = _dot_nt(p[:, 0:PAGE], vp[0][...])
    for i in range(1, npg):
        pv = pv + _dot_nt(p[:, PAGE * i:PAGE * (i + 1)], vp[i][...])
    l_ref[...] = alpha * l_ref[...] + jnp.sum(p, axis=1, keepdims=True)
    acc_ref[...] = alpha * acc_ref[...] + pv
    m_ref[...] = m_new

    @pl.when(j == n_pages // npg - 1)
    def _():
        self_ok = bias_ref[n_pages:n_pages + 1, 0:1] > -1.0
        s_self = jnp.sum(qb.astype(F32) * _mx(kself_ref[...]).astype(F32), axis=1, keepdims=True)
        s_self = jnp.where(self_ok, s_self, NEG)
        m_fin = jnp.maximum(m_new, s_self)
        a_fin = jnp.exp2(m_new - m_fin)
        p_self = jnp.where(self_ok, jnp.exp2(s_self - m_fin), 0.0)
        l = a_fin * l_ref[...] + p_self
        acc = a_fin * acc_ref[...] + _mx(p_self).astype(F32) * _mx(vself_ref[...]).astype(F32)
        o_ref[...] = (acc / l).astype(o_ref.dtype)


def _dsa_sample(q, qi, kiwi, k_new, v_new, cache_k, cache_v, cache_kidx, page_table):
    b, n_pages = page_table.shape
    n_pool = cache_k.shape[0]
    past = n_pages * PAGE
    topk = min(TOPK_MAX, (past + 1) // 4)
    npg = SAMPLE_PAGES_PER_STEP if n_pages % SAMPLE_PAGES_PER_STEP == 0 else n_pages
    pt = page_table.reshape(-1)
    ck = jnp.transpose(cache_k, (0, 2, 3, 1)).reshape(n_pool, KV_W, PAGE)
    cv = jnp.transpose(cache_v, (0, 2, 3, 1)).reshape(n_pool, KV_W, PAGE)
    cidx = jnp.transpose(cache_kidx, (0, 2, 1))
    qi8 = jnp.pad(qi.reshape(b, IDX_HEADS, IDX_DIM), ((0, 0), (0, 8 - IDX_HEADS), (0, 0)))
    wib = jnp.pad(kiwi[:, IDX_DIM:IDX_DIM + IDX_HEADS] * IDX_HEADS ** -0.5, ((0, 0), (0, 8 - IDX_HEADS)))
    wib = jnp.broadcast_to(wib[:, :, None], (b, 8, 128))
    kis = kiwi[:, None, :IDX_DIM]
    per_seq = lambda *shape: pl.BlockSpec((None,) + shape, lambda s, j, pt_: (s,) + (0,) * len(shape))
    page = lambda w, i: pl.BlockSpec((None, w, PAGE), lambda s, j, pt_, i=i: (pt_[s * n_pages + j * npg + i], 0, 0))
    scores = pl.pallas_call(
        functools.partial(_dsa_score_kernel, npg=npg, n_pages=n_pages),
        grid_spec=pltpu.PrefetchScalarGridSpec(
            num_scalar_prefetch=1,
            grid=(b, n_pages // npg),
            in_specs=[per_seq(8, IDX_DIM), per_seq(8, 128), per_seq(1, IDX_DIM)]
                     + [page(IDX_DIM, i) for i in range(npg)],
            out_specs=per_seq(n_pages + 1, 128)),
        out_shape=jax.ShapeDtypeStruct((b, n_pages + 1, 128), F32),
        compiler_params=_cparams("parallel", "arbitrary"),
        name="dsa_sample_score",
    )(pt, qi8, wib, kis, *([cidx] * npg))
    nkeys = (n_pages + 1) * PAGE
    rows = min(b, 64)
    bias = pl.pallas_call(
        functools.partial(_dsa_pick_kernel, topk=topk, rows=rows, nkeys=nkeys),
        grid=(b // rows,),
        in_specs=[pl.BlockSpec((rows, nkeys), lambda i: (i, 0))],
        out_specs=pl.BlockSpec((rows, nkeys), lambda i: (i, 0)),
        out_shape=jax.ShapeDtypeStruct((b, nkeys), F32),
        compiler_params=_cparams("parallel"),
        name="dsa_sample_pick",
    )(scores.reshape(b, nkeys)).reshape(b, n_pages + 1, PAGE)

    hpg = N_HEADS // N_KV
    qh = q.reshape(b, N_HEADS, HEAD_DIM)
    qb = jnp.concatenate(
        [jnp.pad(qh[:, g * hpg:(g + 1) * hpg], ((0, 0), (0, 0), (g * HEAD_DIM, KV_W - (g + 1) * HEAD_DIM)))
         for g in range(N_KV)], axis=1)
    o8 = pl.pallas_call(
        functools.partial(_dsa_att_kernel, npg=npg, n_pages=n_pages),
        grid_spec=pltpu.PrefetchScalarGridSpec(
            num_scalar_prefetch=1,
            grid=(b, n_pages // npg),
            in_specs=[per_seq(8, KV_W), per_seq(n_pages + 1, 128), per_seq(1, KV_W), per_seq(1, KV_W)]
                     + [page(KV_W, i) for i in range(npg)] * 2,
            out_specs=per_seq(8, KV_W),
            scratch_shapes=[pltpu.VMEM((8, 1), F32), pltpu.VMEM((8, 1), F32), pltpu.VMEM((8, KV_W), F32)]),
        out_shape=jax.ShapeDtypeStruct((b, 8, KV_W), F32),
        compiler_params=_cparams("parallel", "arbitrary"),
        name="dsa_sample_attend",
    )(pt, qb, bias, k_new[:, None, :], v_new[:, None, :], *([ck] * npg), *([cv] * npg))
    o = jnp.concatenate([o8[:, h, (h // hpg) * HEAD_DIM:(h // hpg + 1) * HEAD_DIM] for h in range(N_HEADS)], axis=-1)
    return o.astype(MXU_DTYPE)


def _gdn_kernel(cin_ref, gz_ref, gab_ref, cbuf_ref, s0_ref, cw_ref, alog_ref, dtb_ref, gng_ref,
                o_ref, cnew_ref, sfin_ref, tail_ref, s_ref, *, tb, tbv, c, nblk):
    j = pl.program_id(1)
    nc = tb // c

    @pl.when(j == 0)
    def _():
        tail_ref[...] = cbuf_ref[...]
        s_ref[...] = s0_ref[...]

    full = jnp.concatenate([tail_ref[...], cin_ref[...]], axis=0)
    first = 8 - (CONV_W - 1)
    y = full[first:first + tb] * cw_ref[0:1, :]
    for jj in range(1, CONV_W):
        y = y + full[first + jj:first + jj + tb] * cw_ref[jj:jj + 1, :]
    y = _silu(y)
    tail_ref[...] = full[tbv:tbv + 8]

    def per_head(x, width):
        return jnp.concatenate([x[:, width * h:width * (h + 1)].reshape(nc, c, width) for h in range(G_HEADS)], axis=0)

    q = per_head(y[:, :G_KW], G_DK)
    k = per_head(y[:, G_KW:2 * G_KW], G_DK)
    v = per_head(y[:, 2 * G_KW:], G_DV)
    q = q * lax.rsqrt(jnp.sum(q * q, axis=-1, keepdims=True) + EPS) * G_DK ** -0.5
    k = k * lax.rsqrt(jnp.sum(k * k, axis=-1, keepdims=True) + EPS)

    gab = gab_ref[...]
    xa = gab + dtb_ref[...]
    g_t = -jnp.exp(alog_ref[...]) * (jnp.maximum(xa, 0.0) + jnp.log(1.0 + jnp.exp(-jnp.abs(xa))))
    beta_t = 1.0 / (1.0 + jnp.exp(-gab))
    if tbv < tb:
        live = _iota2((tb, 128), 0) < tbv
        g_t = jnp.where(live, g_t, 0.0)
        beta_t = jnp.where(live, beta_t, 0.0)
    lower = _tri(c, "lower")
    ones_lower = jnp.broadcast_to(_mx(jnp.where(lower, 1.0, 0.0)), (nc, c, c))
    ones_upper = jnp.broadcast_to(_mx(jnp.where(_tri(c, "upper"), 1.0, 0.0)), (nc, c, c))
    parts = _split3(g_t.reshape(nc, c, 128))
    gc3 = sum(_bdot(ones_lower, p, 2, 1) for p in parts)
    gcr3 = sum(_bdot(p, ones_upper, 1, 1) for p in parts)
    beta3 = beta_t.reshape(nc, c, 128)
    gc = jnp.concatenate([gc3[:, :, h:h + 1] for h in range(G_HEADS)], axis=0)
    beta = jnp.concatenate([beta3[:, :, G_HEADS + h:G_HEADS + h + 1] for h in range(G_HEADS)], axis=0)
    gc_row = jnp.concatenate([gcr3[:, h:h + 1, :] for h in range(G_HEADS)], axis=0)
    g_last = gc[:, c - 1:c, :]
    decay = jnp.where(lower, jnp.exp(jnp.where(lower, gc - gc_row, 0.0)), 0.0)
    kb = k * beta
    vb = v * beta
    eg = jnp.exp(gc)
    kk = _bdot(jnp.concatenate([kb, q], axis=1), k, 2, 2)
    a_mat = jnp.where(_tri(c, "strict_lower"), kk[:, :c] * decay, 0.0)
    qk = jnp.where(lower, kk[:, c:] * decay, 0.0)
    tinv = jnp.where(_tri(c, "eye"), 1.0, 0.0) - a_mat
    apow = a_mat
    width = 2
    while width < c:
        apow2 = _bdot(apow, apow, 2, 1)
        tinv = tinv + _bdot(tinv, apow2, 2, 1)
        apow = apow2
        width *= 2
    uw = _bdot(tinv, jnp.concatenate([vb, kb * eg], axis=2), 2, 1)
    u = uw[:, :, :G_DV]
    wq = _mx(jnp.concatenate([uw[:, :, G_DV:], q * eg], axis=1))
    kd = _mx(k * jnp.exp(g_last - gc))
    eg_last = jnp.exp(g_last)
    qk = _mx(qk)

    state = s_ref[...]
    outs = []
    for ci in range(nc):
        pick = lambda a: jnp.concatenate([a[h * nc + ci:h * nc + ci + 1] for h in range(G_HEADS)], axis=0)
        ws = _bdot(pick(wq), state, 2, 1)
        v_new = pick(u) - ws[:, :c]
        outs.append(ws[:, c:] + _bdot(pick(qk), v_new, 2, 1))
        state = state * pick(eg_last) + _bdot(pick(kd), v_new, 1, 1)
    s_ref[...] = state

    z = gz_ref[...]
    for h in range(G_HEADS):
        o = jnp.concatenate([outs[ci][h] for ci in range(nc)], axis=0)
        o = o * lax.rsqrt(jnp.mean(o * o, axis=-1, keepdims=True) + EPS) * gng_ref[...]
        o_ref[:, G_DV * h:G_DV * (h + 1)] = (o * _silu(z[:, G_DV * h:G_DV * (h + 1)])).astype(o_ref.dtype)

    @pl.when(j == nblk - 1)
    def _():
        cnew_ref[...] = tail_ref[...]
        sfin_ref[...] = s_ref[...]


def _gdn(cin, gz, gab, cbuf8, s0, conv_w, alog, dtb, gng, t_valid):
    b, t, _ = cin.shape
    if t_valid == t:
        c = min(GDN_CHUNK, t)
        tb = min(ROW_TILE, t)
        tbv = tb
    else:
        c = tb = t
        tbv = t_valid
    nblk = t // tb
    blk = lambda w: pl.BlockSpec((None, tb, w), lambda bb, j: (bb, j, 0))
    per_b = lambda *shape: pl.BlockSpec((None,) + shape, lambda bb, j: (bb,) + (0,) * len(shape))
    full = lambda a: pl.BlockSpec(a.shape, lambda bb, j: (0,) * a.ndim)
    return pl.pallas_call(
        functools.partial(_gdn_kernel, tb=tb, tbv=tbv, c=c, nblk=nblk),
        grid=(b, nblk),
        in_specs=[blk(CONV_DIM), blk(G_VW), blk(128), per_b(8, CONV_DIM), per_b(G_HEADS, G_DK, G_DV),
                  full(conv_w), full(alog), full(dtb), full(gng)],
        out_specs=[blk(G_VW), per_b(8, CONV_DIM), per_b(G_HEADS, G_DK, G_DV)],
        out_shape=[jax.ShapeDtypeStruct((b, t, G_VW), MXU_DTYPE), jax.ShapeDtypeStruct((b, 8, CONV_DIM), F32),
                   jax.ShapeDtypeStruct((b, G_HEADS, G_DK, G_DV), F32)],
        scratch_shapes=[pltpu.VMEM((8, CONV_DIM), F32), pltpu.VMEM((G_HEADS, G_DK, G_DV), F32)],
        compiler_params=_cparams("parallel", "arbitrary"),
        name="gdn",
    )(cin, gz, gab, cbuf8, s0, conv_w, alog, dtb, gng)


def _mem_kv_kernel(x_ref, g_ref, w_ref, kg_ref, mkt_ref, mvt_ref):
    x = x_ref[...]
    h = _mx(x * lax.rsqrt(jnp.mean(x * x, axis=-1, keepdims=True) + EPS) * g_ref[...])
    p = jnp.dot(h, w_ref[...], preferred_element_type=F32)
    mk = p[:, :MEM_W]
    mk = mk * lax.rsqrt(_dot(mk * mk, _group_mean_matrix(MEM_W, MEM_HD)) + EPS) * kg_ref[...]
    mkt_ref[...] = mk.T
    mvt_ref[...] = p[:, MEM_W:].T


def _mem_kv(mem, g, w_kv, kg):
    b, m, _ = mem.shape
    full = lambda a: pl.BlockSpec(a.shape, lambda i: (0,) * a.ndim)
    return pl.pallas_call(
        _mem_kv_kernel,
        grid=(b,),
        in_specs=[pl.BlockSpec((None, m, D_MODEL), lambda i: (i, 0, 0)), full(g), full(w_kv), full(kg)],
        out_specs=[pl.BlockSpec((None, MEM_W, m), lambda i: (i, 0, 0))] * 2,
        out_shape=[jax.ShapeDtypeStruct((b, MEM_W, m), F32)] * 2,
        compiler_params=_cparams("parallel"),
        name="mem_kv",
    )(mem, g, w_kv, kg)


def _mix_out_kernel(x_ref, oa_ref, og_ref, woa_ref, wog_ref, g_ref, wq_ref, qg_ref, x1_ref, qm_ref):
    x1 = (x_ref[...] + jnp.dot(oa_ref[...], woa_ref[...], preferred_element_type=F32)
          + jnp.dot(og_ref[...], wog_ref[...], preferred_element_type=F32))
    x1_ref[...] = x1
    h = _mx(x1 * lax.rsqrt(jnp.mean(x1 * x1, axis=-1, keepdims=True) + EPS) * g_ref[...])
    q = jnp.dot(h, wq_ref[...], preferred_element_type=F32)
    q = q * lax.rsqrt(_dot(q * q, _group_mean_matrix(MEM_W, MEM_HD)) + EPS) * qg_ref[...]
    qm_ref[...] = (q * MEM_HD ** -0.5).astype(qm_ref.dtype)


def _mix_out(x2d, oa, og, w_oa, w_og, g, w_mq, qg, tm):
    n = x2d.shape[0]
    row = lambda w: pl.BlockSpec((tm, w), lambda i: (i, 0))
    full = lambda a: pl.BlockSpec(a.shape, lambda i: (0,) * a.ndim)
    return pl.pallas_call(
        _mix_out_kernel,
        grid=(n // tm,),
        in_specs=[row(D_MODEL), row(ATTN_W), row(G_VW), full(w_oa), full(w_og), full(g), full(w_mq), full(qg)],
        out_specs=[row(D_MODEL), row(MEM_W)],
        out_shape=[jax.ShapeDtypeStruct((n, D_MODEL), F32), jax.ShapeDtypeStruct((n, MEM_W), MXU_DTYPE)],
        compiler_params=_cparams("parallel"),
        name="mix_out",
    )(x2d, oa, og, w_oa, w_og, g, w_mq, qg)


def _mem_attn_kernel(q_ref, mkt_ref, mvt_ref, o_ref):
    for h in range(MEM_HEADS):
        cols = slice(MEM_HD * h, MEM_HD * (h + 1))
        s = _dot(q_ref[:, cols], mkt_ref[cols, :])
        p = jnp.exp(s - jnp.max(s, axis=1, keepdims=True))
        o = _dot_nt(p, mvt_ref[cols, :]) / jnp.sum(p, axis=1, keepdims=True)
        o_ref[:, cols] = o.astype(o_ref.dtype)


def _mem_attn(qm, mkt, mvt, tm):
    b, t, _ = qm.shape
    m = mkt.shape[2]
    return pl.pallas_call(
        _mem_attn_kernel,
        grid=(b, t // tm),
        in_specs=[pl.BlockSpec((None, tm, MEM_W), lambda bb, i: (bb, i, 0)),
                  pl.BlockSpec((None, MEM_W, m), lambda bb, i: (bb, 0, 0)),
                  pl.BlockSpec((None, MEM_W, m), lambda bb, i: (bb, 0, 0))],
        out_specs=pl.BlockSpec((None, tm, MEM_W), lambda bb, i: (bb, i, 0)),
        out_shape=jax.ShapeDtypeStruct((b, t, MEM_W), MXU_DTYPE),
        compiler_params=_cparams("parallel", "arbitrary"),
        name="mem_attn",
    )(qm, mkt, mvt)


def _ffn_kernel(x1_ref, om_ref, wmo_ref, g_ref, wg_ref, wu_ref, wd_ref, y_ref, act_ref, *, fc):
    x2 = x1_ref[...] + jnp.dot(om_ref[...], wmo_ref[...], preferred_element_type=F32)
    h = _mx(x2 * lax.rsqrt(jnp.mean(x2 * x2, axis=-1, keepdims=True) + EPS) * g_ref[...])
    d_ff = wg_ref.shape[1]
    for f0 in range(0, d_ff, fc):
        gate = jnp.dot(h, wg_ref[:, f0:f0 + fc], preferred_element_type=F32)
        up = jnp.dot(h, wu_ref[:, f0:f0 + fc], preferred_element_type=F32)
        act_ref[:, f0:f0 + fc] = (_silu(gate) * up).astype(act_ref.dtype)
    y_ref[...] = x2 + jnp.dot(act_ref[...], wd_ref[...], preferred_element_type=F32)


def _ffn(x1, om, w_mo, g, w_gate, w_up, w_down, tm):
    n = x1.shape[0]
    d_ff = w_gate.shape[1]
    row = lambda w: pl.BlockSpec((tm, w), lambda i: (i, 0))
    full = lambda a: pl.BlockSpec(a.shape, lambda i: (0,) * a.ndim, pipeline_mode=pl.Buffered(1))
    return pl.pallas_call(
        functools.partial(_ffn_kernel, fc=256),
        grid=(n // tm,),
        in_specs=[row(D_MODEL), row(MEM_W), full(w_mo), full(g), full(w_gate), full(w_up), full(w_down)],
        out_specs=row(D_MODEL),
        out_shape=jax.ShapeDtypeStruct((n, D_MODEL), F32),
        scratch_shapes=[pltpu.VMEM((tm, d_ff), MXU_DTYPE)],
        compiler_params=_cparams("parallel"),
        name="ffn",
    )(x1, om, w_mo, g, w_gate, w_up, w_down)


def _prep_weights(attn_norm_g, w_in, q_norm_g, k_norm_g, conv_w, a_log, dt_bias, gdn_norm_g, w_out,
                  xattn_norm_g, w_mq, mq_norm_g, w_mo, ffn_norm_g, w_gate, w_up, w_down):
    lane_pad = lambda a: jnp.pad(a, (0, 128 - a.shape[0]))[None, :]
    return dict(
        attn_g=attn_norm_g[None, :],
        w_dsa=_mx(jnp.pad(w_in[:, :DSA_COLS], ((0, 0), (0, DSA_PAD - DSA_COLS)))),
        w_gdn=_mx(jnp.pad(w_in[:, DSA_COLS:], ((0, 0), (0, GDN_PAD - GDN_COLS)))),
        qg=jnp.tile(q_norm_g, N_HEADS)[None, :],
        kg=jnp.tile(k_norm_g, N_KV)[None, :],
        conv_w=conv_w, alog=lane_pad(a_log), dtb=lane_pad(dt_bias), gng=gdn_norm_g[None, :],
        w_oa=_mx(w_out[:ATTN_W]), w_og=_mx(w_out[ATTN_W:]),
        xattn_g=xattn_norm_g[None, :], w_mq=_mx(w_mq), mqg=jnp.tile(mq_norm_g, MEM_HEADS)[None, :],
        w_mo=_mx(w_mo), ffn_g=ffn_norm_g[None, :], w_gate=_mx(w_gate), w_up=_mx(w_up), w_down=_mx(w_down))


def _trunk_layer(x, mem_kt, mem_vt, conv_buf, ssm0, dsa_sample_fn, w, t_pad):
    b, t, _ = x.shape
    n = b * t
    tm = min(ROW_TILE, n)
    x2d = x.reshape(n, D_MODEL)
    proj = functools.partial(_proj_in, g=w["attn_g"], w_dsa=w["w_dsa"], w_gdn=w["w_gdn"], qg=w["qg"], kg=w["kg"])
    if dsa_sample_fn is None:
        q, kiwi, kt, vt, kit, ktb, va, qi, cin, gz, gab = proj(x, tm=min(ROW_TILE, t), feature_major=True)
        o_attn = _dsa_prompt(q, qi, kiwi, ktb, va, b, t)
        to_tokens = lambda a: jnp.transpose(a.reshape(b, -1, HEAD_DIM, t), (0, 3, 1, 2))
        k_out, v_out, kidx_out = to_tokens(kt), to_tokens(vt), jnp.transpose(kit, (0, 2, 1))
    else:
        q, kiwi, k, v, qi, cin, gz, gab = proj(x2d[None], tm=tm, feature_major=False)
        o_attn = dsa_sample_fn(q, qi, kiwi, k, v)
        k_out, v_out = k.reshape(b, t, N_KV, HEAD_DIM), v.reshape(b, t, N_KV, HEAD_DIM)
        kidx_out = kiwi[:, :IDX_DIM].reshape(b, t, IDX_DIM)
    pad_t = lambda a: jnp.pad(a.reshape(b, t, -1), ((0, 0), (0, t_pad - t), (0, 0)))
    cbuf8 = jnp.pad(conv_buf, ((0, 0), (8 - (CONV_W - 1), 0), (0, 0)))
    o_gdn, cnew8, ssm_new = _gdn(pad_t(cin), pad_t(gz), pad_t(gab), cbuf8, ssm0,
                                 w["conv_w"], w["alog"], w["dtb"], w["gng"], t)
    o_gdn = o_gdn[:, :t].reshape(n, G_VW)
    x1, qm = _mix_out(x2d, o_attn, o_gdn, w["w_oa"], w["w_og"], w["xattn_g"], w["w_mq"], w["mqg"], tm)
    om = _mem_attn(pad_t(qm), mem_kt, mem_vt, min(ROW_TILE, t_pad))[:, :t].reshape(n, MEM_W)
    y = _ffn(x1, om, w["w_mo"], w["ffn_g"], w["w_gate"], w["w_up"], w["w_down"], tm)
    return y.reshape(b, t, D_MODEL), (k_out, v_out, kidx_out, cnew8[:, 8 - (CONV_W - 1):], ssm_new)


def kernel(x_prompt, x_sample, mem_prompt, cache_k, cache_v, cache_kidx, page_table, state_conv, state_ssm,
           cache_mem_k, cache_mem_v, attn_norm_g, w_in, q_norm_g, k_norm_g, conv_w, a_log, dt_bias, gdn_norm_g,
           w_out, xattn_norm_g, mem_norm_g, w_mq, w_mk, w_mv, mq_norm_g, mk_norm_g, w_mo, ffn_norm_g, w_gate,
           w_up, w_down):
    layer_w = (attn_norm_g, w_in, q_norm_g, k_norm_g, conv_w, a_log, dt_bias, gdn_norm_g, w_out,
               xattn_norm_g, w_mq, mq_norm_g, w_mo, ffn_norm_g, w_gate, w_up, w_down)
    depth = w_in.shape[0]
    bp, tp, _ = x_prompt.shape
    bs, ts, _ = x_sample.shape
    assert ts == 1, "the sample group decodes one token per sequence"
    m = mem_prompt.shape[1]
    feature_major = lambda a: jnp.transpose(a, (0, 2, 3, 1)).reshape(a.shape[0], MEM_W, m)
    token_major = lambda a: jnp.transpose(a.reshape(a.shape[0], MEM_HEADS, MEM_HD, m), (0, 3, 1, 2))
    xp, xs = x_prompt, x_sample
    new_p, new_s = [], []
    for l in range(depth):
        w = _prep_weights(*[a[l] for a in layer_w])
        mkt_p, mvt_p = _mem_kv(mem_prompt, mem_norm_g[l][None, :], _mx(jnp.concatenate([w_mk[l], w_mv[l]], axis=1)),
                               jnp.tile(mk_norm_g[l], MEM_HEADS)[None, :])
        xp, st_p = _trunk_layer(xp, mkt_p, mvt_p, jnp.zeros((bp, CONV_W - 1, CONV_DIM), F32),
                                jnp.zeros((bp, G_HEADS, G_DK, G_DV), F32), None, w, tp)
        dsa_s = lambda q, qi, kiwi, k, v: _dsa_sample(q, qi, kiwi, k, v, cache_k[l], cache_v[l],
                                                      cache_kidx[l], page_table)
        xs, st_s = _trunk_layer(xs, feature_major(cache_mem_k[l]), feature_major(cache_mem_v[l]),
                                state_conv[l], state_ssm[l], dsa_s, w, 8)
        new_p.append(st_p + (token_major(mkt_p), token_major(mvt_p)))
        new_s.append(st_s)
    k_p, v_p, kidx_p, conv_p, ssm_p, memk_p, memv_p = [jnp.stack(z) for z in zip(*new_p)]
    k_s, v_s, kidx_s, conv_s, ssm_s = [jnp.stack(z) for z in zip(*new_s)]
    return (xp, xs, k_p, v_p, kidx_p, conv_p, ssm_p, memk_p, memv_p, k_s, v_s, kidx_s, conv_s, ssm_s)
```

```python
import functools

import jax
import jax.numpy as jnp
from jax import lax
from jax.experimental import pallas as pl
from jax.experimental.pallas import tpu as pltpu

F32 = jnp.float32
MXU_DTYPE = jnp.bfloat16

D_MODEL = 1024
PAGE = 128
N_HEADS, N_KV, HEAD_DIM = 8, 2, 64
ATTN_W, KV_W = N_HEADS * HEAD_DIM, N_KV * HEAD_DIM
IDX_HEADS, IDX_DIM = 4, 64
TOPK_MAX = 256
G_HEADS, G_DK, G_DV = 4, 128, 128
G_KW, G_VW = G_HEADS * G_DK, G_HEADS * G_DV
CONV_W = 4
CONV_DIM = 2 * G_KW + G_VW
GDN_CHUNK = 64
MEM_HEADS, MEM_HD = 4, 64
MEM_W = MEM_HEADS * MEM_HD
DSA_COLS = ATTN_W + 2 * KV_W + IDX_HEADS * IDX_DIM + IDX_DIM + IDX_HEADS
DSA_PAD = 1152
GDN_COLS = 2 * G_KW + 2 * G_VW + 2 * G_HEADS
GDN_PAD = 2176
EPS = 1e-6
NEG = -1e30
INT_MIN = -2 ** 31
KEY_NEG_INF = -2139095041
FLT_MAX = 3.4028234663852886e38
VMEM_LIMIT = 56 * 1024 * 1024
ROW_TILE = 512
DSA_Q_BLOCK = 256
DSA_KEY_CHUNK = 256
BISECT_STEPS_PER_CHECK = 4
LOG2E = 1.4426950408889634


def _cparams(*sem):
    return pltpu.CompilerParams(dimension_semantics=sem, vmem_limit_bytes=VMEM_LIMIT)


def _mx(a):
    return a.astype(MXU_DTYPE)


def _dot(a, b):
    return jnp.dot(_mx(a), _mx(b), preferred_element_type=F32)


def _dot_nt(a, b):
    return lax.dot_general(_mx(a), _mx(b), (((1,), (1,)), ((), ())), preferred_element_type=F32)


def _bdot(a, b, ca, cb):
    return lax.dot_general(_mx(a), _mx(b), (((ca,), (cb,)), ((0,), (0,))), preferred_element_type=F32)


def _split3(a):
    hi = _mx(a)
    r1 = a - hi.astype(F32)
    mid = _mx(r1)
    lo = _mx(r1 - mid.astype(F32))
    return hi, mid, lo


def _iota2(shape, dim):
    return lax.broadcasted_iota(jnp.int32, shape, dim)


def _tri(n, kind):
    r, c = _iota2((n, n), 0), _iota2((n, n), 1)
    return {"upper": r <= c, "lower": r >= c, "strict_lower": r > c, "eye": r == c}[kind]


def _group_mean_matrix(n, group):
    shift = group.bit_length() - 1
    r, c = _iota2((n, n), 0) >> shift, _iota2((n, n), 1) >> shift
    return jnp.where(r == c, 1.0 / group, 0.0).astype(MXU_DTYPE)


def _silu(x):
    half = 0.5 * x
    return half + half * jnp.tanh(half)


def _key_to_float(key):
    bits = jnp.where(key < 0, key ^ jnp.int32(0x7FFFFFFF), key)
    return jnp.where(key < KEY_NEG_INF, -jnp.inf, lax.bitcast_convert_type(bits, F32))


def _next_candidate(it, base):
    return jnp.where(it == 0, jnp.zeros_like(base), base | lax.shift_left(jnp.int32(1), 31 - it))


def _proj_in_kernel(x_ref, g_ref, wd_ref, wg_ref, qg_ref, kg_ref, *outs, feature_major):
    if feature_major:
        q_ref, kiwi_ref, kt_ref, vt_ref, kit_ref, ktb_ref, va_ref, qi_ref, cin_ref, gz_ref, gab_ref = outs
    else:
        q_ref, kiwi_ref, k_ref, v_ref, qi_ref, cin_ref, gz_ref, gab_ref = outs
    x = x_ref[...]
    h = _mx(x * lax.rsqrt(jnp.mean(x * x, axis=-1, keepdims=True) + EPS) * g_ref[...])
    pd = jnp.dot(h, wd_ref[...], preferred_element_type=F32)
    q = pd[:, :ATTN_W]
    k = pd[:, ATTN_W:ATTN_W + KV_W]
    v = pd[:, ATTN_W + KV_W:ATTN_W + 2 * KV_W]
    qi = pd[:, ATTN_W + 2 * KV_W:ATTN_W + 2 * KV_W + IDX_HEADS * IDX_DIM]
    kiwi = pd[:, 1024:1152]
    m64 = _group_mean_matrix(ATTN_W, HEAD_DIM)
    q = q * lax.rsqrt(_dot(q * q, m64) + EPS) * qg_ref[...]
    k = k * lax.rsqrt(_dot(k * k, m64[:KV_W, :KV_W]) + EPS) * kg_ref[...]
    q_ref[...] = (q * (HEAD_DIM ** -0.5 * LOG2E)).astype(q_ref.dtype)
    kiwi_ref[...] = kiwi
    qi_ref[...] = (qi * IDX_DIM ** -0.5).astype(qi_ref.dtype)
    if feature_major:
        kt, vt, kit = k.T, v.T, kiwi.T[:IDX_DIM]
        kt_ref[...] = kt
        vt_ref[...] = vt
        kit_ref[...] = kit
        ktb_ref[0:KV_W, :] = kt.astype(ktb_ref.dtype)
        ktb_ref[KV_W:KV_W + IDX_DIM, :] = kit.astype(ktb_ref.dtype)
        lane = _iota2(v.shape, 1)
        for g, vg in enumerate((v, jnp.concatenate([v[:, HEAD_DIM:], v[:, :HEAD_DIM]], axis=1))):
            aug = jnp.where(lane < HEAD_DIM, vg, jnp.where(lane == HEAD_DIM, 1.0, 0.0))
            va_ref[:, 128 * g:128 * (g + 1)] = aug.astype(va_ref.dtype)
    else:
        k_ref[...] = k
        v_ref[...] = v
    pg = jnp.dot(h, wg_ref[...], preferred_element_type=F32)
    cin_ref[...] = pg[:, :CONV_DIM]
    gz_ref[...] = pg[:, CONV_DIM:CONV_DIM + G_VW]
    gab_ref[...] = pg[:, CONV_DIM + G_VW:GDN_PAD]


def _proj_in(x, g, w_dsa, w_gdn, qg, kg, tm, feature_major):
    b, t, _ = x.shape
    nt = t // tm
    row = lambda w: pl.BlockSpec((tm, w), lambda bb, i: (bb * nt + i, 0))
    col = lambda w: pl.BlockSpec((None, w, tm), lambda bb, i: (bb, 0, i))
    full = lambda a: pl.BlockSpec(a.shape, lambda bb, i: (0,) * a.ndim)
    rows = lambda w, dt: (row(w), jax.ShapeDtypeStruct((b * t, w), dt))
    cols = lambda w, dt: (col(w), jax.ShapeDtypeStruct((b, w, t), dt))
    outs = [rows(ATTN_W, MXU_DTYPE), rows(128, F32)]
    if feature_major:
        outs += [cols(KV_W, F32), cols(KV_W, F32), cols(IDX_DIM, F32), cols(KV_W + IDX_DIM, MXU_DTYPE),
                 (pl.BlockSpec((None, tm, 256), lambda bb, i: (bb, i, 0)), jax.ShapeDtypeStruct((b, t, 256), MXU_DTYPE))]
    else:
        outs += [rows(KV_W, F32), rows(KV_W, F32)]
    outs += [rows(IDX_HEADS * IDX_DIM, MXU_DTYPE), rows(CONV_DIM, F32), rows(G_VW, F32), rows(128, F32)]
    return pl.pallas_call(
        functools.partial(_proj_in_kernel, feature_major=feature_major),
        grid=(b, nt),
        in_specs=[pl.BlockSpec((None, tm, D_MODEL), lambda bb, i: (bb, i, 0)),
                  full(g), full(w_dsa), full(w_gdn), full(qg), full(kg)],
        out_specs=[o[0] for o in outs],
        out_shape=[o[1] for o in outs],
        compiler_params=_cparams("parallel", "parallel"),
        name="proj_in",
    )(x, g, w_dsa, w_gdn, qg, kg)


def _dsa_prompt_kernel(q_ref, qi_ref, kiwi_ref, kt_ref, va_ref, o_ref, s_ref, bias_ref, *, topk, tq, nkeys, row0):
    kit = kt_ref[KV_W:KV_W + IDX_DIM, :]
    wi = kiwi_ref[:, IDX_DIM:IDX_DIM + IDX_HEADS] * IDX_HEADS ** -0.5
    score = None
    for h in range(IDX_HEADS):
        rel = jnp.maximum(_dot(qi_ref[:, IDX_DIM * h:IDX_DIM * (h + 1)], kit), 0.0)
        term = wi[:, h:h + 1] * rel
        score = term if score is None else score + term
    adm = _iota2((tq, nkeys), 1) <= _iota2((tq, nkeys), 0) + row0
    s_ref[...] = jnp.where(adm, score, -jnp.inf)
    _topk_bias(s_ref, bias_ref, topk, tq, nkeys)
    kc = min(DSA_KEY_CHUNK, nkeys)
    for g in range(N_KV):
        for r in range(N_HEADS // N_KV):
            h = g * (N_HEADS // N_KV) + r
            qh = q_ref[:, HEAD_DIM * h:HEAD_DIM * (h + 1)]
            mm = jnp.full((tq, 128), NEG, F32)
            for c0 in range(0, nkeys, kc):
                ks = slice(c0, c0 + kc)
                sc = _dot(qh, kt_ref[HEAD_DIM * g:HEAD_DIM * (g + 1), ks]) + bias_ref[:, ks]
                s_ref[:, ks] = sc
                for l0 in range(0, kc, 128):
                    mm = jnp.maximum(mm, sc[:, l0:l0 + 128])
            m = jnp.max(mm, axis=1, keepdims=True)
            acc = jnp.zeros((tq, 128), F32)
            for c0 in range(0, nkeys, kc):
                ks = slice(c0, c0 + kc)
                p = jnp.exp2(_mx(s_ref[:, ks] - m))
                acc = acc + jnp.dot(p, va_ref[ks, 128 * g:128 * (g + 1)], preferred_element_type=F32)
            o_ref[:, HEAD_DIM * h:HEAD_DIM * (h + 1)] = (
                acc[:, :HEAD_DIM] / acc[:, HEAD_DIM:HEAD_DIM + 1]).astype(o_ref.dtype)


def _topk_bias(s_ref, bias_ref, topk, tq, nkeys):
    halves = 2
    hr = tq // halves
    rows = [slice(p * hr, (p + 1) * hr) for p in range(halves)]
    n_adm = [jnp.sum(jnp.where(s_ref[r, :] > -jnp.inf, 1.0, 0.0), axis=1, keepdims=True) for r in rows]

    def pending(cnts):
        worst = [jnp.max(jnp.where(n > float(topk), jnp.abs(c - float(topk)), 0.0)) for n, c in zip(n_adm, cnts)]
        return functools.reduce(jnp.maximum, worst) > 0.0

    def body(state):
        it, _, bases, cnts = state
        bases, cnts = list(bases), list(cnts)
        for j in range(BISECT_STEPS_PER_CHECK):
            for p, r in enumerate(rows):
                cand = _next_candidate(it + j, bases[p])
                cnt = jnp.sum(jnp.where(s_ref[r, :] >= _key_to_float(cand), 1.0, 0.0), axis=1, keepdims=True)
                ok = cnt >= float(topk)
                bases[p] = jnp.where(ok, cand, bases[p])
                cnts[p] = jnp.where(ok, cnt, cnts[p])
        return it + BISECT_STEPS_PER_CHECK, pending(cnts), tuple(bases), tuple(cnts)

    cnts0 = tuple(jnp.full((hr, 1), float(nkeys), F32) for _ in rows)
    bases0 = tuple(jnp.full((hr, 1), INT_MIN, jnp.int32) for _ in rows)
    _, _, bases, _ = lax.while_loop(lambda st: (st[0] < 32) & st[1], body,
                                    (jnp.int32(0), pending(cnts0), bases0, cnts0))
    thr = jnp.maximum(_key_to_float(jnp.concatenate(bases, axis=0)), -FLT_MAX)
    thr = jnp.where(jnp.concatenate(n_adm, axis=0) <= float(topk), -FLT_MAX, thr)
    s = s_ref[...]
    gt = s > thr
    tie = s == thr
    need = float(topk) - jnp.sum(jnp.where(gt, 1.0, 0.0), axis=1, keepdims=True)
    n_tie = jnp.sum(jnp.where(tie, 1.0, 0.0), axis=1, keepdims=True)
    crowded = jnp.max(n_tie - need) > 0.0

    @pl.when(jnp.logical_not(crowded))
    def _():
        bias_ref[...] = jnp.where(s >= thr, 0.0, NEG)

    @pl.when(crowded)
    def _():
        tie_f = _mx(jnp.where(tie, 1.0, 0.0))
        upper = _mx(jnp.where(_tri(128, "upper"), 1.0, 0.0))
        off = jnp.zeros((tq, 1), F32)
        parts = []
        for c in range(nkeys // 128):
            pc = jnp.dot(tie_f[:, 128 * c:128 * (c + 1)], upper, preferred_element_type=F32) + off
            off = pc[:, 127:128]
            parts.append(pc)
        rank = jnp.concatenate(parts, axis=1)
        bias_ref[...] = jnp.where(gt | (tie & (rank <= need)), 0.0, NEG)


def _dsa_prompt(q, qi, kiwi, ktb, va, b, t):
    tq = min(DSA_Q_BLOCK, t)
    topk = min(TOPK_MAX, t // 4)
    nq = t // tq
    outs = []
    for i in range(nq):
        nkeys = (i + 1) * tq
        blk = lambda w, i=i: pl.BlockSpec((tq, w), lambda bb: (bb * nq + i, 0))
        outs.append(pl.pallas_call(
            functools.partial(_dsa_prompt_kernel, topk=topk, tq=tq, nkeys=nkeys, row0=i * tq),
            grid=(b,),
            in_specs=[blk(ATTN_W), blk(IDX_HEADS * IDX_DIM), blk(128),
                      pl.BlockSpec((None, KV_W + IDX_DIM, nkeys), lambda bb: (bb, 0, 0)),
                      pl.BlockSpec((None, nkeys, 256), lambda bb: (bb, 0, 0))],
            out_specs=pl.BlockSpec((None, tq, ATTN_W), lambda bb: (bb, 0, 0)),
            out_shape=jax.ShapeDtypeStruct((b, tq, ATTN_W), MXU_DTYPE),
            scratch_shapes=[pltpu.VMEM((tq, nkeys), F32), pltpu.VMEM((tq, nkeys), F32)],
            compiler_params=_cparams("parallel"),
            name=f"dsa_prompt_{i}",
        )(q, qi, kiwi, ktb, va))
    return jnp.concatenate(outs, axis=1).reshape(b * t, ATTN_W)


def _page_gather(pt_ref, pool_hbm, buf, sem, n_pages):
    s = pl.program_id(0)
    slot = lax.rem(s, 2)

    def fetch(seq, into):
        def one(p, carry):
            pltpu.make_async_copy(pool_hbm.at[pt_ref[seq * n_pages + p]], buf.at[into, p], sem.at[into]).start()
            return carry
        lax.fori_loop(0, n_pages, one, 0)

    @pl.when(s == 0)
    def _():
        fetch(0, 0)

    @pl.when(s + 1 < pl.num_programs(0))
    def _():
        fetch(s + 1, 1 - slot)

    pltpu.make_async_copy(pool_hbm.at[pl.ds(0, n_pages)], buf.at[slot], sem.at[slot]).wait()
    return slot


def _dsa_score_kernel(pt_ref, qi_ref, wib_ref, kis_ref, cidx_hbm, sc_ref, buf, sem, *, n_pages):
    slot = _page_gather(pt_ref, cidx_hbm, buf, sem, n_pages)
    qi = qi_ref[...]
    w = wib_ref[...][:, 0:1]
    z = _bdot(jnp.broadcast_to(qi[None], (n_pages,) + qi.shape), buf[slot], 2, 1)
    sc_ref[0:n_pages] = jnp.sum(w * jnp.maximum(z, 0.0), axis=1, keepdims=True)
    z_self = jnp.sum(qi.astype(F32) * _mx(kis_ref[...]).astype(F32), axis=1, keepdims=True)
    s_self = jnp.sum(w * jnp.maximum(z_self, 0.0), axis=0, keepdims=True)
    sc_ref[n_pages] = jnp.where(_iota2((1, 128), 1) == 0, s_self, -jnp.inf)


def _dsa_pick_kernel(s_ref, bias_ref, *, topk, rows, nkeys):
    _topk_bias(s_ref, bias_ref, topk, rows, nkeys)


def _dsa_att_kernel(pt_ref, qb_ref, bias_ref, kself_ref, vself_ref, ck_hbm, cv_hbm, o_ref,
                    kbuf, vbuf, ksem, vsem, *, n_pages):
    slot = _page_gather(pt_ref, ck_hbm, kbuf, ksem, n_pages)
    _page_gather(pt_ref, cv_hbm, vbuf, vsem, n_pages)
    qb = qb_ref[...]
    qbb = jnp.broadcast_to(qb[None], (n_pages,) + qb.shape)
    s = _bdot(qbb, kbuf[slot], 2, 1) + bias_ref[0:n_pages]
    self_ok = bias_ref[n_pages][:, 0:1] > -1.0
    s_self = jnp.sum(qb.astype(F32) * _mx(kself_ref[...]).astype(F32), axis=1, keepdims=True)
    s_self = jnp.where(self_ok, s_self, NEG)
    m = jnp.maximum(jnp.max(jnp.max(s, axis=2, keepdims=True), axis=0), s_self)
    p = jnp.where(s > 0.5 * NEG, jnp.exp2(s - m), 0.0)
    p_self = jnp.where(self_ok, jnp.exp2(s_self - m), 0.0)
    l = jnp.sum(jnp.sum(p, axis=2, keepdims=True), axis=0) + p_self
    pv = jnp.sum(_bdot(p, vbuf[slot], 2, 2), axis=0)
    acc = pv + _mx(p_self).astype(F32) * _mx(vself_ref[...]).astype(F32)
    o_ref[...] = (acc / l).astype(o_ref.dtype)


def _dsa_sample(q, qi, kiwi, k_new, v_new, cache_k, cache_v, cache_kidx, page_table):
    b, n_pages = page_table.shape
    n_pool = cache_k.shape[0]
    past = n_pages * PAGE
    topk = min(TOPK_MAX, (past + 1) // 4)
    pt = page_table.reshape(-1)
    ck = jnp.transpose(cache_k, (0, 2, 3, 1)).reshape(n_pool, KV_W, PAGE)
    cv = jnp.transpose(cache_v, (0, 2, 3, 1)).reshape(n_pool, KV_W, PAGE)
    cidx = jnp.transpose(cache_kidx, (0, 2, 1))
    qi8 = jnp.pad(qi.reshape(b, IDX_HEADS, IDX_DIM), ((0, 0), (0, 8 - IDX_HEADS), (0, 0)))
    wib = jnp.pad(kiwi[:, IDX_DIM:IDX_DIM + IDX_HEADS] * IDX_HEADS ** -0.5, ((0, 0), (0, 8 - IDX_HEADS)))
    wib = jnp.broadcast_to(wib[:, :, None], (b, 8, 128))
    kis = kiwi[:, None, :IDX_DIM]
    per_seq = lambda *shape: pl.BlockSpec((None,) + shape, lambda s, pt_: (s,) + (0,) * len(shape))
    hbm = pl.BlockSpec(memory_space=pl.ANY)
    page_buf = lambda rows: [pltpu.VMEM((2, n_pages, rows, PAGE), F32), pltpu.SemaphoreType.DMA((2,))]
    scores = pl.pallas_call(
        functools.partial(_dsa_score_kernel, n_pages=n_pages),
        grid_spec=pltpu.PrefetchScalarGridSpec(
            num_scalar_prefetch=1,
            grid=(b,),
            in_specs=[per_seq(8, IDX_DIM), per_seq(8, 128), per_seq(1, IDX_DIM), hbm],
            out_specs=per_seq(n_pages + 1, 1, 128),
            scratch_shapes=page_buf(IDX_DIM)),
        out_shape=jax.ShapeDtypeStruct((b, n_pages + 1, 1, 128), F32),
        compiler_params=_cparams("arbitrary"),
        name="dsa_sample_score",
    )(pt, qi8, wib, kis, cidx)
    nkeys = (n_pages + 1) * PAGE
    rows = min(b, 64)
    bias = pl.pallas_call(
        functools.partial(_dsa_pick_kernel, topk=topk, rows=rows, nkeys=nkeys),
        grid=(b // rows,),
        in_specs=[pl.BlockSpec((rows, nkeys), lambda i: (i, 0))],
        out_specs=pl.BlockSpec((rows, nkeys), lambda i: (i, 0)),
        out_shape=jax.ShapeDtypeStruct((b, nkeys), F32),
        compiler_params=_cparams("parallel"),
        name="dsa_sample_pick",
    )(scores.reshape(b, nkeys)).reshape(b, n_pages + 1, 1, PAGE)

    hpg = N_HEADS // N_KV
    qh = q.reshape(b, N_HEADS, HEAD_DIM)
    qb = jnp.concatenate(
        [jnp.pad(qh[:, g * hpg:(g + 1) * hpg], ((0, 0), (0, 0), (g * HEAD_DIM, KV_W - (g + 1) * HEAD_DIM)))
         for g in range(N_KV)], axis=1)
    o8 = pl.pallas_call(
        functools.partial(_dsa_att_kernel, n_pages=n_pages),
        grid_spec=pltpu.PrefetchScalarGridSpec(
            num_scalar_prefetch=1,
            grid=(b,),
            in_specs=[per_seq(8, KV_W), per_seq(n_pages + 1, 1, 128), per_seq(1, KV_W), per_seq(1, KV_W), hbm, hbm],
            out_specs=per_seq(8, KV_W),
            scratch_shapes=[s for pair in zip(page_buf(KV_W), page_buf(KV_W)) for s in pair]),
        out_shape=jax.ShapeDtypeStruct((b, 8, KV_W), F32),
        compiler_params=_cparams("arbitrary"),
        name="dsa_sample_attend",
    )(pt, qb, bias, k_new[:, None, :], v_new[:, None, :], ck, cv)
    o = jnp.concatenate([o8[:, h, (h // hpg) * HEAD_DIM:(h // hpg + 1) * HEAD_DIM] for h in range(N_HEADS)], axis=-1)
    return o.astype(MXU_DTYPE)


def _gdn_kernel(cin_ref, gz_ref, gab_ref, cbuf_ref, s0_ref, cw_ref, alog_ref, dtb_ref, gng_ref,
                o_ref, cnew_ref, sfin_ref, tail_ref, s_ref, *, tb, tbv, c, nblk):
    j = pl.program_id(1)
    nc = tb // c

    @pl.when(j == 0)
    def _():
        tail_ref[...] = cbuf_ref[...]
        s_ref[...] = s0_ref[...]

    full = jnp.concatenate([tail_ref[...], cin_ref[...]], axis=0)
    first = 8 - (CONV_W - 1)
    y = full[first:first + tb] * cw_ref[0:1, :]
    for jj in range(1, CONV_W):
        y = y + full[first + jj:first + jj + tb] * cw_ref[jj:jj + 1, :]
    y = _silu(y)
    tail_ref[...] = full[tbv:tbv + 8]

    def per_head(x, width):
        return jnp.concatenate([x[:, width * h:width * (h + 1)].reshape(nc, c, width) for h in range(G_HEADS)], axis=0)

    q = per_head(y[:, :G_KW], G_DK)
    k = per_head(y[:, G_KW:2 * G_KW], G_DK)
    v = per_head(y[:, 2 * G_KW:], G_DV)
    ones_dk = jnp.ones((G_DK, G_DK), MXU_DTYPE)

    def sumsq(a):
        a2 = (a * a).reshape(G_HEADS * tb, G_DK)
        return jnp.dot(_mx(a2), ones_dk, preferred_element_type=F32).reshape(a.shape)
    q = q * (lax.rsqrt(sumsq(q) + EPS) * G_DK ** -0.5)
    k = k * lax.rsqrt(sumsq(k) + EPS)

    gab = gab_ref[...]
    xa = gab + dtb_ref[...]
    g_t = -jnp.exp(alog_ref[...]) * (jnp.maximum(xa, 0.0) + jnp.log(1.0 + jnp.exp(-jnp.abs(xa))))
    beta_t = 1.0 / (1.0 + jnp.exp(-gab))
    if tbv < tb:
        live = _iota2((tb, 128), 0) < tbv
        g_t = jnp.where(live, g_t, 0.0)
        beta_t = jnp.where(live, beta_t, 0.0)
    lower = _tri(c, "lower")
    ones_lower = jnp.broadcast_to(_mx(jnp.where(lower, 1.0, 0.0)), (nc, c, c))
    ones_upper = jnp.broadcast_to(_mx(jnp.where(_tri(c, "upper"), 1.0, 0.0)), (nc, c, c))
    parts = _split3(g_t.reshape(nc, c, 128))
    gc3 = sum(_bdot(ones_lower, p, 2, 1) for p in parts)
    gcr3 = sum(_bdot(p, ones_upper, 1, 1) for p in parts)
    beta3 = beta_t.reshape(nc, c, 128)
    gc = jnp.concatenate([gc3[:, :, h:h + 1] for h in range(G_HEADS)], axis=0)
    beta = jnp.concatenate([beta3[:, :, G_HEADS + h:G_HEADS + h + 1] for h in range(G_HEADS)], axis=0)
    gc_row = jnp.concatenate([gcr3[:, h:h + 1, :] for h in range(G_HEADS)], axis=0)
    g_last = gc[:, c - 1:c, :]
    decay = jnp.where(lower, jnp.exp(jnp.where(lower, gc - gc_row, 0.0)), 0.0)
    kb = k * beta
    vb = v * beta
    eg = jnp.exp(gc)
    kk = _bdot(jnp.concatenate([kb, q], axis=1), k, 2, 2)
    a_mat = jnp.where(_tri(c, "strict_lower"), kk[:, :c] * decay, 0.0)
    qk = jnp.where(lower, kk[:, c:] * decay, 0.0)
    tinv = jnp.where(_tri(c, "eye"), 1.0, 0.0) - a_mat
    apow = a_mat
    width = 2
    while width < c:
        apow2 = _bdot(apow, apow, 2, 1)
        tinv = tinv + _bdot(tinv, apow2, 2, 1)
        apow = apow2
        width *= 2
    uw = _bdot(tinv, jnp.concatenate([vb, kb * eg], axis=2), 2, 1)
    u = uw[:, :, :G_DV]
    wq = _mx(jnp.concatenate([uw[:, :, G_DV:], q * eg], axis=1))
    kd = _mx(k * jnp.exp(g_last - gc))
    eg_last = jnp.exp(g_last)
    qk = _mx(qk)

    state = s_ref[...]
    outs = []
    for ci in range(nc):
        pick = lambda a: jnp.concatenate([a[h * nc + ci:h * nc + ci + 1] for h in range(G_HEADS)], axis=0)
        ws = _bdot(pick(wq), state, 2, 1)
        v_new = pick(u) - ws[:, :c]
        outs.append(ws[:, c:] + _bdot(pick(qk), v_new, 2, 1))
        state = state * pick(eg_last) + _bdot(pick(kd), v_new, 1, 1)
    s_ref[...] = state

    z = gz_ref[...]
    for h in range(G_HEADS):
        o = jnp.concatenate([outs[ci][h] for ci in range(nc)], axis=0)
        o = o * lax.rsqrt(jnp.mean(o * o, axis=-1, keepdims=True) + EPS) * gng_ref[...]
        o_ref[:, G_DV * h:G_DV * (h + 1)] = (o * _silu(z[:, G_DV * h:G_DV * (h + 1)])).astype(o_ref.dtype)

    @pl.when(j == nblk - 1)
    def _():
        cnew_ref[...] = tail_ref[...]
        sfin_ref[...] = s_ref[...]


def _gdn(cin, gz, gab, cbuf8, s0, conv_w, alog, dtb, gng, t_valid):
    b, t, _ = cin.shape
    if t_valid == t:
        c = min(GDN_CHUNK, t)
        tb = min(ROW_TILE, t)
        tbv = tb
    else:
        c = tb = t
        tbv = t_valid
    nblk = t // tb
    blk = lambda w: pl.BlockSpec((None, tb, w), lambda bb, j: (bb, j, 0))
    per_b = lambda *shape: pl.BlockSpec((None,) + shape, lambda bb, j: (bb,) + (0,) * len(shape))
    full = lambda a: pl.BlockSpec(a.shape, lambda bb, j: (0,) * a.ndim)
    return pl.pallas_call(
        functools.partial(_gdn_kernel, tb=tb, tbv=tbv, c=c, nblk=nblk),
        grid=(b, nblk),
        in_specs=[blk(CONV_DIM), blk(G_VW), blk(128), per_b(8, CONV_DIM), per_b(G_HEADS, G_DK, G_DV),
                  full(conv_w), full(alog), full(dtb), full(gng)],
        out_specs=[blk(G_VW), per_b(8, CONV_DIM), per_b(G_HEADS, G_DK, G_DV)],
        out_shape=[jax.ShapeDtypeStruct((b, t, G_VW), MXU_DTYPE), jax.ShapeDtypeStruct((b, 8, CONV_DIM), F32),
                   jax.ShapeDtypeStruct((b, G_HEADS, G_DK, G_DV), F32)],
        scratch_shapes=[pltpu.VMEM((8, CONV_DIM), F32), pltpu.VMEM((G_HEADS, G_DK, G_DV), F32)],
        compiler_params=_cparams("parallel", "arbitrary"),
        name="gdn",
    )(cin, gz, gab, cbuf8, s0, conv_w, alog, dtb, gng)


def _mem_kv_kernel(x_ref, g_ref, w_ref, kg_ref, mkt_ref, mvt_ref):
    x = x_ref[...]
    h = _mx(x * lax.rsqrt(jnp.mean(x * x, axis=-1, keepdims=True) + EPS) * g_ref[...])
    p = jnp.dot(h, w_ref[...], preferred_element_type=F32)
    mk = p[:, :MEM_W]
    mk = mk * lax.rsqrt(_dot(mk * mk, _group_mean_matrix(MEM_W, MEM_HD)) + EPS) * kg_ref[...]
    mkt_ref[...] = mk.T
    mvt_ref[...] = p[:, MEM_W:].T


def _mem_kv(mem, g, w_kv, kg):
    b, m, _ = mem.shape
    full = lambda a: pl.BlockSpec(a.shape, lambda i: (0,) * a.ndim)
    return pl.pallas_call(
        _mem_kv_kernel,
        grid=(b,),
        in_specs=[pl.BlockSpec((None, m, D_MODEL), lambda i: (i, 0, 0)), full(g), full(w_kv), full(kg)],
        out_specs=[pl.BlockSpec((None, MEM_W, m), lambda i: (i, 0, 0))] * 2,
        out_shape=[jax.ShapeDtypeStruct((b, MEM_W, m), F32)] * 2,
        compiler_params=_cparams("parallel"),
        name="mem_kv",
    )(mem, g, w_kv, kg)


def _mix_out_kernel(x_ref, oa_ref, og_ref, woa_ref, wog_ref, g_ref, wq_ref, qg_ref, x1_ref, qm_ref):
    x1 = (x_ref[...] + jnp.dot(oa_ref[...], woa_ref[...], preferred_element_type=F32)
          + jnp.dot(og_ref[...], wog_ref[...], preferred_element_type=F32))
    x1_ref[...] = x1
    h = _mx(x1 * lax.rsqrt(jnp.mean(x1 * x1, axis=-1, keepdims=True) + EPS) * g_ref[...])
    q = jnp.dot(h, wq_ref[...], preferred_element_type=F32)
    q = q * lax.rsqrt(_dot(q * q, _group_mean_matrix(MEM_W, MEM_HD)) + EPS) * qg_ref[...]
    qm_ref[...] = (q * MEM_HD ** -0.5).astype(qm_ref.dtype)


def _mix_out(x2d, oa, og, w_oa, w_og, g, w_mq, qg, tm):
    n = x2d.shape[0]
    row = lambda w: pl.BlockSpec((tm, w), lambda i: (i, 0))
    full = lambda a: pl.BlockSpec(a.shape, lambda i: (0,) * a.ndim)
    return pl.pallas_call(
        _mix_out_kernel,
        grid=(n // tm,),
        in_specs=[row(D_MODEL), row(ATTN_W), row(G_VW), full(w_oa), full(w_og), full(g), full(w_mq), full(qg)],
        out_specs=[row(D_MODEL), row(MEM_W)],
        out_shape=[jax.ShapeDtypeStruct((n, D_MODEL), F32), jax.ShapeDtypeStruct((n, MEM_W), MXU_DTYPE)],
        compiler_params=_cparams("parallel"),
        name="mix_out",
    )(x2d, oa, og, w_oa, w_og, g, w_mq, qg)


def _mem_attn_kernel(q_ref, mkt_ref, mvt_ref, o_ref):
    for h in range(MEM_HEADS):
        cols = slice(MEM_HD * h, MEM_HD * (h + 1))
        s = _dot(q_ref[:, cols], mkt_ref[cols, :])
        p = jnp.exp(s - jnp.max(s, axis=1, keepdims=True))
        o = _dot_nt(p, mvt_ref[cols, :]) / jnp.sum(p, axis=1, keepdims=True)
        o_ref[:, cols] = o.astype(o_ref.dtype)


def _mem_attn(qm, mkt, mvt, tm):
    b, t, _ = qm.shape
    m = mkt.shape[2]
    return pl.pallas_call(
        _mem_attn_kernel,
        grid=(b, t // tm),
        in_specs=[pl.BlockSpec((None, tm, MEM_W), lambda bb, i: (bb, i, 0)),
                  pl.BlockSpec((None, MEM_W, m), lambda bb, i: (bb, 0, 0)),
                  pl.BlockSpec((None, MEM_W, m), lambda bb, i: (bb, 0, 0))],
        out_specs=pl.BlockSpec((None, tm, MEM_W), lambda bb, i: (bb, i, 0)),
        out_shape=jax.ShapeDtypeStruct((b, t, MEM_W), MXU_DTYPE),
        compiler_params=_cparams("parallel", "arbitrary"),
        name="mem_attn",
    )(qm, mkt, mvt)


def _ffn_kernel(x1_ref, om_ref, wmo_ref, g_ref, wg_ref, wu_ref, wd_ref, y_ref, act_ref, *, fc):
    x2 = x1_ref[...] + jnp.dot(om_ref[...], wmo_ref[...], preferred_element_type=F32)
    h = _mx(x2 * lax.rsqrt(jnp.mean(x2 * x2, axis=-1, keepdims=True) + EPS) * g_ref[...])
    d_ff = wg_ref.shape[1]
    for f0 in range(0, d_ff, fc):
        gate = jnp.dot(h, wg_ref[:, f0:f0 + fc], preferred_element_type=F32)
        up = jnp.dot(h, wu_ref[:, f0:f0 + fc], preferred_element_type=F32)
        act_ref[:, f0:f0 + fc] = (_silu(gate) * up).astype(act_ref.dtype)
    y_ref[...] = x2 + jnp.dot(act_ref[...], wd_ref[...], preferred_element_type=F32)


def _ffn(x1, om, w_mo, g, w_gate, w_up, w_down, tm):
    n = x1.shape[0]
    d_ff = w_gate.shape[1]
    row = lambda w: pl.BlockSpec((tm, w), lambda i: (i, 0))
    full = lambda a: pl.BlockSpec(a.shape, lambda i: (0,) * a.ndim, pipeline_mode=pl.Buffered(1))
    return pl.pallas_call(
        functools.partial(_ffn_kernel, fc=256),
        grid=(n // tm,),
        in_specs=[row(D_MODEL), row(MEM_W), full(w_mo), full(g), full(w_gate), full(w_up), full(w_down)],
        out_specs=row(D_MODEL),
        out_shape=jax.ShapeDtypeStruct((n, D_MODEL), F32),
        scratch_shapes=[pltpu.VMEM((tm, d_ff), MXU_DTYPE)],
        compiler_params=_cparams("parallel"),
        name="ffn",
    )(x1, om, w_mo, g, w_gate, w_up, w_down)


def _prep_weights(attn_norm_g, w_in, q_norm_g, k_norm_g, conv_w, a_log, dt_bias, gdn_norm_g, w_out,
                  xattn_norm_g, w_mq, mq_norm_g, w_mo, ffn_norm_g, w_gate, w_up, w_down):
    lane_pad = lambda a: jnp.pad(a, (0, 128 - a.shape[0]))[None, :]
    return dict(
        attn_g=attn_norm_g[None, :],
        w_dsa=_mx(jnp.pad(w_in[:, :DSA_COLS], ((0, 0), (0, DSA_PAD - DSA_COLS)))),
        w_gdn=_mx(jnp.pad(w_in[:, DSA_COLS:], ((0, 0), (0, GDN_PAD - GDN_COLS)))),
        qg=jnp.tile(q_norm_g, N_HEADS)[None, :],
        kg=jnp.tile(k_norm_g, N_KV)[None, :],
        conv_w=conv_w, alog=lane_pad(a_log), dtb=lane_pad(dt_bias), gng=gdn_norm_g[None, :],
        w_oa=_mx(w_out[:ATTN_W]), w_og=_mx(w_out[ATTN_W:]),
        xattn_g=xattn_norm_g[None, :], w_mq=_mx(w_mq), mqg=jnp.tile(mq_norm_g, MEM_HEADS)[None, :],
        w_mo=_mx(w_mo), ffn_g=ffn_norm_g[None, :], w_gate=_mx(w_gate), w_up=_mx(w_up), w_down=_mx(w_down))


def _trunk_layer(x, mem_kt, mem_vt, conv_buf, ssm0, dsa_sample_fn, w, t_pad):
    b, t, _ = x.shape
    n = b * t
    tm = min(ROW_TILE, n)
    x2d = x.reshape(n, D_MODEL)
    proj = functools.partial(_proj_in, g=w["attn_g"], w_dsa=w["w_dsa"], w_gdn=w["w_gdn"], qg=w["qg"], kg=w["kg"])
    if dsa_sample_fn is None:
        q, kiwi, kt, vt, kit, ktb, va, qi, cin, gz, gab = proj(x, tm=min(ROW_TILE, t), feature_major=True)
        o_attn = _dsa_prompt(q, qi, kiwi, ktb, va, b, t)
        to_tokens = lambda a: jnp.transpose(a.reshape(b, -1, HEAD_DIM, t), (0, 3, 1, 2))
        k_out, v_out, kidx_out = to_tokens(kt), to_tokens(vt), jnp.transpose(kit, (0, 2, 1))
    else:
        q, kiwi, k, v, qi, cin, gz, gab = proj(x2d[None], tm=tm, feature_major=False)
        o_attn = dsa_sample_fn(q, qi, kiwi, k, v)
        k_out, v_out = k.reshape(b, t, N_KV, HEAD_DIM), v.reshape(b, t, N_KV, HEAD_DIM)
        kidx_out = kiwi[:, :IDX_DIM].reshape(b, t, IDX_DIM)
    pad_t = lambda a: jnp.pad(a.reshape(b, t, -1), ((0, 0), (0, t_pad - t), (0, 0)))
    cbuf8 = jnp.pad(conv_buf, ((0, 0), (8 - (CONV_W - 1), 0), (0, 0)))
    o_gdn, cnew8, ssm_new = _gdn(pad_t(cin), pad_t(gz), pad_t(gab), cbuf8, ssm0,
                                 w["conv_w"], w["alog"], w["dtb"], w["gng"], t)
    o_gdn = o_gdn[:, :t].reshape(n, G_VW)
    x1, qm = _mix_out(x2d, o_attn, o_gdn, w["w_oa"], w["w_og"], w["xattn_g"], w["w_mq"], w["mqg"], tm)
    om = _mem_attn(pad_t(qm), mem_kt, mem_vt, min(ROW_TILE, t_pad))[:, :t].reshape(n, MEM_W)
    y = _ffn(x1, om, w["w_mo"], w["ffn_g"], w["w_gate"], w["w_up"], w["w_down"], tm)
    return y.reshape(b, t, D_MODEL), (k_out, v_out, kidx_out, cnew8[:, 8 - (CONV_W - 1):], ssm_new)


def kernel(x_prompt, x_sample, mem_prompt, cache_k, cache_v, cache_kidx, page_table, state_conv, state_ssm,
           cache_mem_k, cache_mem_v, attn_norm_g, w_in, q_norm_g, k_norm_g, conv_w, a_log, dt_bias, gdn_norm_g,
           w_out, xattn_norm_g, mem_norm_g, w_mq, w_mk, w_mv, mq_norm_g, mk_norm_g, w_mo, ffn_norm_g, w_gate,
           w_up, w_down):
    layer_w = (attn_norm_g, w_in, q_norm_g, k_norm_g, conv_w, a_log, dt_bias, gdn_norm_g, w_out,
               xattn_norm_g, w_mq, mq_norm_g, w_mo, ffn_norm_g, w_gate, w_up, w_down)
    depth = w_in.shape[0]
    bp, tp, _ = x_prompt.shape
    bs, ts, _ = x_sample.shape
    assert ts == 1, "the sample group decodes one token per sequence"
    m = mem_prompt.shape[1]
    feature_major = lambda a: jnp.transpose(a, (0, 2, 3, 1)).reshape(a.shape[0], MEM_W, m)
    token_major = lambda a: jnp.transpose(a.reshape(a.shape[0], MEM_HEADS, MEM_HD, m), (0, 3, 1, 2))
    xp, xs = x_prompt, x_sample
    new_p, new_s = [], []
    for l in range(depth):
        w = _prep_weights(*[a[l] for a in layer_w])
        mkt_p, mvt_p = _mem_kv(mem_prompt, mem_norm_g[l][None, :], _mx(jnp.concatenate([w_mk[l], w_mv[l]], axis=1)),
                               jnp.tile(mk_norm_g[l], MEM_HEADS)[None, :])
        xp, st_p = _trunk_layer(xp, mkt_p, mvt_p, jnp.zeros((bp, CONV_W - 1, CONV_DIM), F32),
                                jnp.zeros((bp, G_HEADS, G_DK, G_DV), F32), None, w, tp)
        dsa_s = lambda q, qi, kiwi, k, v: _dsa_sample(q, qi, kiwi, k, v, cache_k[l], cache_v[l],
                                                      cache_kidx[l], page_table)
        xs, st_s = _trunk_layer(xs, feature_major(cache_mem_k[l]), feature_major(cache_mem_v[l]),
                                state_conv[l], state_ssm[l], dsa_s, w, 8)
        new_p.append(st_p + (token_major(mkt_p), token_major(mvt_p)))
        new_s.append(st_s)
    k_p, v_p, kidx_p, conv_p, ssm_p, memk_p, memv_p = [jnp.stack(z) for z in zip(*new_p)]
    k_s, v_s, kidx_s, conv_s, ssm_s = [jnp.stack(z) for z in zip(*new_s)]
    return (xp, xs, k_p, v_p, kidx_p, conv_p, ssm_p, memk_p, memv_p, k_s, v_s, kidx_s, conv_s, ssm_s)
```

```python
import functools

import jax
import jax.numpy as jnp
from jax import lax
from jax.experimental import pallas as pl
from jax.experimental.pallas import tpu as pltpu

F32 = jnp.float32
MXU_DTYPE = jnp.bfloat16

D_MODEL = 1024
PAGE = 128
N_HEADS, N_KV, HEAD_DIM = 8, 2, 64
ATTN_W, KV_W = N_HEADS * HEAD_DIM, N_KV * HEAD_DIM
IDX_HEADS, IDX_DIM = 4, 64
TOPK_MAX = 256
G_HEADS, G_DK, G_DV = 4, 128, 128
G_KW, G_VW = G_HEADS * G_DK, G_HEADS * G_DV
CONV_W = 4
CONV_DIM = 2 * G_KW + G_VW
GDN_CHUNK = 64
MEM_HEADS, MEM_HD = 4, 64
MEM_W = MEM_HEADS * MEM_HD
DSA_COLS = ATTN_W + 2 * KV_W + IDX_HEADS * IDX_DIM + IDX_DIM + IDX_HEADS
DSA_PAD = 1152
GDN_COLS = 2 * G_KW + 2 * G_VW + 2 * G_HEADS
GDN_PAD = 2176
EPS = 1e-6
NEG = -1e30
INT_MIN = -2 ** 31
KEY_NEG_INF = -2139095041
FLT_MAX = 3.4028234663852886e38
VMEM_LIMIT = 56 * 1024 * 1024
ROW_TILE = 512
DSA_Q_BLOCK = 256
DSA_KEY_CHUNK = 256
BISECT_STEPS_PER_CHECK = 4
GDN_SEQS_PER_STEP = 2
GDN_SHORT_SEQS_PER_STEP = 8
MEM_SHORT_SEQS_PER_STEP = 8
LOG2E = 1.4426950408889634


def _cparams(*sem):
    return pltpu.CompilerParams(dimension_semantics=sem, vmem_limit_bytes=VMEM_LIMIT)


def _mx(a):
    return a.astype(MXU_DTYPE)


def _dot(a, b):
    return jnp.dot(_mx(a), _mx(b), preferred_element_type=F32)


def _dot_nt(a, b):
    return lax.dot_general(_mx(a), _mx(b), (((1,), (1,)), ((), ())), preferred_element_type=F32)


def _bdot(a, b, ca, cb):
    return lax.dot_general(_mx(a), _mx(b), (((ca,), (cb,)), ((0,), (0,))), preferred_element_type=F32)


def _split3(a):
    hi = _mx(a)
    r1 = a - hi.astype(F32)
    mid = _mx(r1)
    lo = _mx(r1 - mid.astype(F32))
    return hi, mid, lo


def _iota2(shape, dim):
    return lax.broadcasted_iota(jnp.int32, shape, dim)


def _tri(n, kind):
    r, c = _iota2((n, n), 0), _iota2((n, n), 1)
    return {"upper": r <= c, "lower": r >= c, "strict_lower": r > c, "eye": r == c}[kind]


def _group_mean_matrix(n, group):
    shift = group.bit_length() - 1
    r, c = _iota2((n, n), 0) >> shift, _iota2((n, n), 1) >> shift
    return jnp.where(r == c, 1.0 / group, 0.0).astype(MXU_DTYPE)


def _silu(x):
    half = 0.5 * x
    return half + half * jnp.tanh(half)


def _key_to_float(key):
    bits = jnp.where(key < 0, key ^ jnp.int32(0x7FFFFFFF), key)
    return jnp.where(key < KEY_NEG_INF, -jnp.inf, lax.bitcast_convert_type(bits, F32))


def _next_candidate(it, base):
    return jnp.where(it == 0, jnp.zeros_like(base), base | lax.shift_left(jnp.int32(1), 31 - it))


def _proj_in_kernel(x_ref, g_ref, wd_ref, wg_ref, qg_ref, kg_ref, *outs, feature_major):
    if feature_major:
        q_ref, kiwi_ref, kt_ref, vt_ref, kit_ref, ktb_ref, va_ref, qi_ref, cin_ref, gz_ref, gab_ref = outs
    else:
        q_ref, kiwi_ref, k_ref, v_ref, qi_ref, cin_ref, gz_ref, gab_ref = outs
    x = x_ref[...]
    h = _mx(x * lax.rsqrt(jnp.mean(x * x, axis=-1, keepdims=True) + EPS) * g_ref[...])
    pd = jnp.dot(h, wd_ref[...], preferred_element_type=F32)
    q = pd[:, :ATTN_W]
    k = pd[:, ATTN_W:ATTN_W + KV_W]
    v = pd[:, ATTN_W + KV_W:ATTN_W + 2 * KV_W]
    qi = pd[:, ATTN_W + 2 * KV_W:ATTN_W + 2 * KV_W + IDX_HEADS * IDX_DIM]
    kiwi = pd[:, 1024:1152]
    m64 = _group_mean_matrix(ATTN_W, HEAD_DIM)
    q = q * lax.rsqrt(_dot(q * q, m64) + EPS) * qg_ref[...]
    k = k * lax.rsqrt(_dot(k * k, m64[:KV_W, :KV_W]) + EPS) * kg_ref[...]
    q_ref[...] = (q * (HEAD_DIM ** -0.5 * LOG2E)).astype(q_ref.dtype)
    kiwi_ref[...] = kiwi
    qi_ref[...] = (qi * IDX_DIM ** -0.5).astype(qi_ref.dtype)
    if feature_major:
        kt, vt, kit = k.T, v.T, kiwi.T[:IDX_DIM]
        kt_ref[...] = kt
        vt_ref[...] = vt
        kit_ref[...] = kit
        ktb_ref[0:KV_W, :] = kt.astype(ktb_ref.dtype)
        ktb_ref[KV_W:KV_W + IDX_DIM, :] = kit.astype(ktb_ref.dtype)
        lane = _iota2(v.shape, 1)
        for g, vg in enumerate((v, jnp.concatenate([v[:, HEAD_DIM:], v[:, :HEAD_DIM]], axis=1))):
            aug = jnp.where(lane < HEAD_DIM, vg, jnp.where(lane == HEAD_DIM, 1.0, 0.0))
            va_ref[:, 128 * g:128 * (g + 1)] = aug.astype(va_ref.dtype)
    else:
        k_ref[...] = k
        v_ref[...] = v
    pg = jnp.dot(h, wg_ref[...], preferred_element_type=F32)
    cin_ref[...] = pg[:, :CONV_DIM]
    gz_ref[...] = pg[:, CONV_DIM:CONV_DIM + G_VW]
    gab_ref[...] = pg[:, CONV_DIM + G_VW:GDN_PAD]


def _proj_in(x, g, w_dsa, w_gdn, qg, kg, tm, feature_major):
    b, t, _ = x.shape
    nt = t // tm
    row = lambda w: pl.BlockSpec((tm, w), lambda bb, i: (bb * nt + i, 0))
    col = lambda w: pl.BlockSpec((None, w, tm), lambda bb, i: (bb, 0, i))
    full = lambda a: pl.BlockSpec(a.shape, lambda bb, i: (0,) * a.ndim)
    rows = lambda w, dt: (row(w), jax.ShapeDtypeStruct((b * t, w), dt))
    cols = lambda w, dt: (col(w), jax.ShapeDtypeStruct((b, w, t), dt))
    outs = [rows(ATTN_W, MXU_DTYPE), rows(128, F32)]
    if feature_major:
        outs += [cols(KV_W, F32), cols(KV_W, F32), cols(IDX_DIM, F32), cols(KV_W + IDX_DIM, MXU_DTYPE),
                 (pl.BlockSpec((None, tm, 256), lambda bb, i: (bb, i, 0)), jax.ShapeDtypeStruct((b, t, 256), MXU_DTYPE))]
    else:
        outs += [rows(KV_W, F32), rows(KV_W, F32)]
    outs += [rows(IDX_HEADS * IDX_DIM, MXU_DTYPE), rows(CONV_DIM, F32), rows(G_VW, F32), rows(128, F32)]
    return pl.pallas_call(
        functools.partial(_proj_in_kernel, feature_major=feature_major),
        grid=(b, nt),
        in_specs=[pl.BlockSpec((None, tm, D_MODEL), lambda bb, i: (bb, i, 0)),
                  full(g), full(w_dsa), full(w_gdn), full(qg), full(kg)],
        out_specs=[o[0] for o in outs],
        out_shape=[o[1] for o in outs],
        compiler_params=_cparams("parallel", "parallel"),
        name="proj_in",
    )(x, g, w_dsa, w_gdn, qg, kg)


def _dsa_prompt_kernel(q_ref, qi_ref, kiwi_ref, kt_ref, va_ref, o_ref, s_ref, bias_ref, *, topk, tq, nkeys, row0):
    kit = kt_ref[KV_W:KV_W + IDX_DIM, :]
    wi = kiwi_ref[:, IDX_DIM:IDX_DIM + IDX_HEADS] * IDX_HEADS ** -0.5
    score = None
    for h in range(IDX_HEADS):
        rel = jnp.maximum(_dot(qi_ref[:, IDX_DIM * h:IDX_DIM * (h + 1)], kit), 0.0)
        term = wi[:, h:h + 1] * rel
        score = term if score is None else score + term
    adm = _iota2((tq, nkeys), 1) <= _iota2((tq, nkeys), 0) + row0
    s_ref[...] = jnp.where(adm, score, -jnp.inf)
    _topk_bias(s_ref, bias_ref, topk, tq, nkeys)
    kc = min(DSA_KEY_CHUNK, nkeys)
    for g in range(N_KV):
        for r in range(N_HEADS // N_KV):
            h = g * (N_HEADS // N_KV) + r
            qh = q_ref[:, HEAD_DIM * h:HEAD_DIM * (h + 1)]
            mm = jnp.full((tq, 128), NEG, F32)
            for c0 in range(0, nkeys, kc):
                ks = slice(c0, c0 + kc)
                sc = _dot(qh, kt_ref[HEAD_DIM * g:HEAD_DIM * (g + 1), ks]) + bias_ref[:, ks]
                s_ref[:, ks] = sc
                for l0 in range(0, kc, 128):
                    mm = jnp.maximum(mm, sc[:, l0:l0 + 128])
            m = jnp.max(mm, axis=1, keepdims=True)
            acc = jnp.zeros((tq, 128), F32)
            for c0 in range(0, nkeys, kc):
                ks = slice(c0, c0 + kc)
                p = jnp.exp2(_mx(s_ref[:, ks] - m))
                acc = acc + jnp.dot(p, va_ref[ks, 128 * g:128 * (g + 1)], preferred_element_type=F32)
            o_ref[:, HEAD_DIM * h:HEAD_DIM * (h + 1)] = (
                acc[:, :HEAD_DIM] / acc[:, HEAD_DIM:HEAD_DIM + 1]).astype(o_ref.dtype)


def _topk_bias(s_ref, bias_ref, topk, tq, nkeys):
    halves = 2
    hr = tq // halves
    rows = [slice(p * hr, (p + 1) * hr) for p in range(halves)]
    n_adm = [jnp.sum(jnp.where(s_ref[r, :] > -jnp.inf, 1.0, 0.0), axis=1, keepdims=True) for r in rows]

    def pending(cnts):
        worst = [jnp.max(jnp.where(n > float(topk), jnp.abs(c - float(topk)), 0.0)) for n, c in zip(n_adm, cnts)]
        return functools.reduce(jnp.maximum, worst) > 0.0

    def body(state):
        it, _, bases, cnts = state
        bases, cnts = list(bases), list(cnts)
        for j in range(BISECT_STEPS_PER_CHECK):
            for p, r in enumerate(rows):
                cand = _next_candidate(it + j, bases[p])
                cnt = jnp.sum(jnp.where(s_ref[r, :] >= _key_to_float(cand), 1.0, 0.0), axis=1, keepdims=True)
                ok = cnt >= float(topk)
                bases[p] = jnp.where(ok, cand, bases[p])
                cnts[p] = jnp.where(ok, cnt, cnts[p])
        return it + BISECT_STEPS_PER_CHECK, pending(cnts), tuple(bases), tuple(cnts)

    cnts0 = tuple(jnp.full((hr, 1), float(nkeys), F32) for _ in rows)
    bases0 = tuple(jnp.full((hr, 1), INT_MIN, jnp.int32) for _ in rows)
    _, _, bases, _ = lax.while_loop(lambda st: (st[0] < 32) & st[1], body,
                                    (jnp.int32(0), pending(cnts0), bases0, cnts0))
    thr = jnp.maximum(_key_to_float(jnp.concatenate(bases, axis=0)), -FLT_MAX)
    thr = jnp.where(jnp.concatenate(n_adm, axis=0) <= float(topk), -FLT_MAX, thr)
    s = s_ref[...]
    gt = s > thr
    tie = s == thr
    need = float(topk) - jnp.sum(jnp.where(gt, 1.0, 0.0), axis=1, keepdims=True)
    n_tie = jnp.sum(jnp.where(tie, 1.0, 0.0), axis=1, keepdims=True)
    crowded = jnp.max(n_tie - need) > 0.0

    @pl.when(jnp.logical_not(crowded))
    def _():
        bias_ref[...] = jnp.where(s >= thr, 0.0, NEG)

    @pl.when(crowded)
    def _():
        tie_f = _mx(jnp.where(tie, 1.0, 0.0))
        upper = _mx(jnp.where(_tri(128, "upper"), 1.0, 0.0))
        off = jnp.zeros((tq, 1), F32)
        parts = []
        for c in range(nkeys // 128):
            pc = jnp.dot(tie_f[:, 128 * c:128 * (c + 1)], upper, preferred_element_type=F32) + off
            off = pc[:, 127:128]
            parts.append(pc)
        rank = jnp.concatenate(parts, axis=1)
        bias_ref[...] = jnp.where(gt | (tie & (rank <= need)), 0.0, NEG)


def _dsa_prompt(q, qi, kiwi, ktb, va, b, t):
    tq = min(DSA_Q_BLOCK, t)
    topk = min(TOPK_MAX, t // 4)
    nq = t // tq
    outs = []
    for i in range(nq):
        nkeys = (i + 1) * tq
        blk = lambda w, i=i: pl.BlockSpec((tq, w), lambda bb: (bb * nq + i, 0))
        outs.append(pl.pallas_call(
            functools.partial(_dsa_prompt_kernel, topk=topk, tq=tq, nkeys=nkeys, row0=i * tq),
            grid=(b,),
            in_specs=[blk(ATTN_W), blk(IDX_HEADS * IDX_DIM), blk(128),
                      pl.BlockSpec((None, KV_W + IDX_DIM, nkeys), lambda bb: (bb, 0, 0)),
                      pl.BlockSpec((None, nkeys, 256), lambda bb: (bb, 0, 0))],
            out_specs=pl.BlockSpec((None, tq, ATTN_W), lambda bb: (bb, 0, 0)),
            out_shape=jax.ShapeDtypeStruct((b, tq, ATTN_W), MXU_DTYPE),
            scratch_shapes=[pltpu.VMEM((tq, nkeys), F32), pltpu.VMEM((tq, nkeys), F32)],
            compiler_params=_cparams("parallel"),
            name=f"dsa_prompt_{i}",
        )(q, qi, kiwi, ktb, va))
    return jnp.concatenate(outs, axis=1).reshape(b * t, ATTN_W)


def _page_gather(pt_ref, pool_hbm, buf, sem, n_pages):
    s = pl.program_id(0)
    slot = lax.rem(s, 2)

    def fetch(seq, into):
        def one(p, carry):
            pltpu.make_async_copy(pool_hbm.at[pt_ref[seq * n_pages + p]], buf.at[into, p], sem.at[into]).start()
            return carry
        lax.fori_loop(0, n_pages, one, 0)

    @pl.when(s == 0)
    def _():
        fetch(0, 0)

    @pl.when(s + 1 < pl.num_programs(0))
    def _():
        fetch(s + 1, 1 - slot)

    pltpu.make_async_copy(pool_hbm.at[pl.ds(0, n_pages)], buf.at[slot], sem.at[slot]).wait()
    return slot


def _dsa_score_kernel(pt_ref, qi_ref, wib_ref, kis_ref, cidx_hbm, sc_ref, buf, sem, *, n_pages):
    slot = _page_gather(pt_ref, cidx_hbm, buf, sem, n_pages)
    qi = qi_ref[...]
    w = wib_ref[...][:, 0:1]
    z = _bdot(jnp.broadcast_to(qi[None], (n_pages,) + qi.shape), buf[slot], 2, 1)
    sc_ref[0:n_pages] = jnp.sum(w * jnp.maximum(z, 0.0), axis=1, keepdims=True)
    z_self = jnp.sum(qi.astype(F32) * _mx(kis_ref[...]).astype(F32), axis=1, keepdims=True)
    s_self = jnp.sum(w * jnp.maximum(z_self, 0.0), axis=0, keepdims=True)
    sc_ref[n_pages] = jnp.where(_iota2((1, 128), 1) == 0, s_self, -jnp.inf)


def _dsa_pick_kernel(s_ref, bias_ref, *, topk, rows, nkeys):
    _topk_bias(s_ref, bias_ref, topk, rows, nkeys)


def _dsa_att_kernel(pt_ref, qb_ref, bias_ref, kself_ref, vself_ref, ck_hbm, cv_hbm, o_ref,
                    kbuf, vbuf, ksem, vsem, *, n_pages):
    slot = _page_gather(pt_ref, ck_hbm, kbuf, ksem, n_pages)
    _page_gather(pt_ref, cv_hbm, vbuf, vsem, n_pages)
    qb = qb_ref[...]
    qbb = jnp.broadcast_to(qb[None], (n_pages,) + qb.shape)
    s = _bdot(qbb, kbuf[slot], 2, 1) + bias_ref[0:n_pages]
    self_ok = bias_ref[n_pages][:, 0:1] > -1.0
    s_self = jnp.sum(qb.astype(F32) * _mx(kself_ref[...]).astype(F32), axis=1, keepdims=True)
    s_self = jnp.where(self_ok, s_self, NEG)
    m = jnp.maximum(jnp.max(jnp.max(s, axis=2, keepdims=True), axis=0), s_self)
    p = jnp.where(s > 0.5 * NEG, jnp.exp2(s - m), 0.0)
    p_self = jnp.where(self_ok, jnp.exp2(s_self - m), 0.0)
    l = jnp.sum(jnp.sum(p, axis=2, keepdims=True), axis=0) + p_self
    pv = jnp.sum(_bdot(p, vbuf[slot], 2, 2), axis=0)
    acc = pv + _mx(p_self).astype(F32) * _mx(vself_ref[...]).astype(F32)
    o_ref[...] = (acc / l).astype(o_ref.dtype)


def _dsa_sample(q, qi, kiwi, k_new, v_new, cache_k, cache_v, cache_kidx, page_table):
    b, n_pages = page_table.shape
    n_pool = cache_k.shape[0]
    past = n_pages * PAGE
    topk = min(TOPK_MAX, (past + 1) // 4)
    pt = page_table.reshape(-1)
    ck = jnp.transpose(cache_k, (0, 2, 3, 1)).reshape(n_pool, KV_W, PAGE)
    cv = jnp.transpose(cache_v, (0, 2, 3, 1)).reshape(n_pool, KV_W, PAGE)
    cidx = jnp.transpose(cache_kidx, (0, 2, 1))
    qi8 = jnp.pad(qi.reshape(b, IDX_HEADS, IDX_DIM), ((0, 0), (0, 8 - IDX_HEADS), (0, 0)))
    wib = jnp.pad(kiwi[:, IDX_DIM:IDX_DIM + IDX_HEADS] * IDX_HEADS ** -0.5, ((0, 0), (0, 8 - IDX_HEADS)))
    wib = jnp.broadcast_to(wib[:, :, None], (b, 8, 128))
    kis = kiwi[:, None, :IDX_DIM]
    per_seq = lambda *shape: pl.BlockSpec((None,) + shape, lambda s, pt_: (s,) + (0,) * len(shape))
    hbm = pl.BlockSpec(memory_space=pl.ANY)
    page_buf = lambda rows: [pltpu.VMEM((2, n_pages, rows, PAGE), F32), pltpu.SemaphoreType.DMA((2,))]
    scores = pl.pallas_call(
        functools.partial(_dsa_score_kernel, n_pages=n_pages),
        grid_spec=pltpu.PrefetchScalarGridSpec(
            num_scalar_prefetch=1,
            grid=(b,),
            in_specs=[per_seq(8, IDX_DIM), per_seq(8, 128), per_seq(1, IDX_DIM), hbm],
            out_specs=per_seq(n_pages + 1, 1, 128),
            scratch_shapes=page_buf(IDX_DIM)),
        out_shape=jax.ShapeDtypeStruct((b, n_pages + 1, 1, 128), F32),
        compiler_params=_cparams("arbitrary"),
        name="dsa_sample_score",
    )(pt, qi8, wib, kis, cidx)
    nkeys = (n_pages + 1) * PAGE
    rows = min(b, 64)
    bias = pl.pallas_call(
        functools.partial(_dsa_pick_kernel, topk=topk, rows=rows, nkeys=nkeys),
        grid=(b // rows,),
        in_specs=[pl.BlockSpec((rows, nkeys), lambda i: (i, 0))],
        out_specs=pl.BlockSpec((rows, nkeys), lambda i: (i, 0)),
        out_shape=jax.ShapeDtypeStruct((b, nkeys), F32),
        compiler_params=_cparams("parallel"),
        name="dsa_sample_pick",
    )(scores.reshape(b, nkeys)).reshape(b, n_pages + 1, 1, PAGE)

    hpg = N_HEADS // N_KV
    qh = q.reshape(b, N_HEADS, HEAD_DIM)
    qb = jnp.concatenate(
        [jnp.pad(qh[:, g * hpg:(g + 1) * hpg], ((0, 0), (0, 0), (g * HEAD_DIM, KV_W - (g + 1) * HEAD_DIM)))
         for g in range(N_KV)], axis=1)
    o8 = pl.pallas_call(
        functools.partial(_dsa_att_kernel, n_pages=n_pages),
        grid_spec=pltpu.PrefetchScalarGridSpec(
            num_scalar_prefetch=1,
            grid=(b,),
            in_specs=[per_seq(8, KV_W), per_seq(n_pages + 1, 1, 128), per_seq(1, KV_W), per_seq(1, KV_W), hbm, hbm],
            out_specs=per_seq(8, KV_W),
            scratch_shapes=[s for pair in zip(page_buf(KV_W), page_buf(KV_W)) for s in pair]),
        out_shape=jax.ShapeDtypeStruct((b, 8, KV_W), F32),
        compiler_params=_cparams("arbitrary"),
        name="dsa_sample_attend",
    )(pt, qb, bias, k_new[:, None, :], v_new[:, None, :], ck, cv)
    o = jnp.concatenate([o8[:, h, (h // hpg) * HEAD_DIM:(h // hpg + 1) * HEAD_DIM] for h in range(N_HEADS)], axis=-1)
    return o.astype(MXU_DTYPE)


def _gdn_kernel(cin_ref, gz_ref, gab_ref, cbuf_ref, s0_ref, cw_ref, alog_ref, dtb_ref, gng_ref,
                o_ref, cnew_ref, sfin_ref, tail_ref, s_ref, *, nb, tb, tbv, c, nblk):
    j = pl.program_id(1)
    nc = tb // c
    ng = nb * G_HEADS

    @pl.when(j == 0)
    def _():
        tail_ref[...] = cbuf_ref[...]
        s_ref[...] = s0_ref[...]

    first = 8 - (CONV_W - 1)
    ys = []
    for s in range(nb):
        full = jnp.concatenate([tail_ref[s], cin_ref[s]], axis=0)
        y = full[first:first + tb] * cw_ref[0:1, :]
        for jj in range(1, CONV_W):
            y = y + full[first + jj:first + jj + tb] * cw_ref[jj:jj + 1, :]
        ys.append(_silu(y))
        tail_ref[s] = full[tbv:tbv + 8]

    def per_group(off, width):
        return jnp.concatenate([y[:, off + width * h:off + width * (h + 1)].reshape(nc, c, width)
                                for y in ys for h in range(G_HEADS)], axis=0)

    q = per_group(0, G_DK)
    k = per_group(G_KW, G_DK)
    v = per_group(2 * G_KW, G_DV)
    ones_dk = jnp.ones((G_DK, G_DK), MXU_DTYPE)

    def sumsq(a):
        a2 = (a * a).reshape(ng * tb, G_DK)
        return jnp.dot(_mx(a2), ones_dk, preferred_element_type=F32).reshape(a.shape)
    q = q * (lax.rsqrt(sumsq(q) + EPS) * G_DK ** -0.5)
    k = k * lax.rsqrt(sumsq(k) + EPS)

    gab = gab_ref[...].reshape(nb * tb, 128)
    xa = gab + dtb_ref[...]
    g_t = -jnp.exp(alog_ref[...]) * (jnp.maximum(xa, 0.0) + jnp.log(1.0 + jnp.exp(-jnp.abs(xa))))
    beta_t = 1.0 / (1.0 + jnp.exp(-gab))
    if tbv < tb:
        live = (_iota2((nb * tb, 128), 0) & (tb - 1)) < tbv
        g_t = jnp.where(live, g_t, 0.0)
        beta_t = jnp.where(live, beta_t, 0.0)
    lower = _tri(c, "lower")
    ones_lower = jnp.broadcast_to(_mx(jnp.where(lower, 1.0, 0.0)), (nb * nc, c, c))
    ones_upper = jnp.broadcast_to(_mx(jnp.where(_tri(c, "upper"), 1.0, 0.0)), (nb * nc, c, c))
    parts = _split3(g_t.reshape(nb * nc, c, 128))
    gc3 = sum(_bdot(ones_lower, p, 2, 1) for p in parts)
    gcr3 = sum(_bdot(p, ones_upper, 1, 1) for p in parts)
    beta3 = beta_t.reshape(nb * nc, c, 128)
    groups = [(slice(s * nc, (s + 1) * nc), h) for s in range(nb) for h in range(G_HEADS)]
    gc = jnp.concatenate([gc3[r, :, h:h + 1] for r, h in groups], axis=0)
    beta = jnp.concatenate([beta3[r, :, G_HEADS + h:G_HEADS + h + 1] for r, h in groups], axis=0)
    gc_row = jnp.concatenate([gcr3[r, h:h + 1, :] for r, h in groups], axis=0)
    g_last = gc[:, c - 1:c, :]
    decay = jnp.where(lower, jnp.exp(jnp.where(lower, gc - gc_row, 0.0)), 0.0)
    kb = k * beta
    vb = v * beta
    eg = jnp.exp(gc)
    kk = _bdot(jnp.concatenate([kb, q], axis=1), k, 2, 2)
    a_mat = jnp.where(_tri(c, "strict_lower"), kk[:, :c] * decay, 0.0)
    qk = jnp.where(lower, kk[:, c:] * decay, 0.0)
    tinv = jnp.where(_tri(c, "eye"), 1.0, 0.0) - a_mat
    apow = a_mat
    width = 2
    while width < c:
        apow2 = _bdot(apow, apow, 2, 1)
        tinv = tinv + _bdot(tinv, apow2, 2, 1)
        apow = apow2
        width *= 2
    uw = _bdot(tinv, jnp.concatenate([vb, kb * eg], axis=2), 2, 1)
    u = uw[:, :, :G_DV]
    wq = _mx(jnp.concatenate([uw[:, :, G_DV:], q * eg], axis=1))
    kd = _mx(k * jnp.exp(g_last - gc))
    eg_last = jnp.exp(g_last)
    qk = _mx(qk)

    state = s_ref[...].reshape(ng, G_DK, G_DV)
    outs = []
    for ci in range(nc):
        pick = lambda a: jnp.concatenate([a[g * nc + ci:g * nc + ci + 1] for g in range(ng)], axis=0)
        ws = _bdot(pick(wq), state, 2, 1)
        v_new = pick(u) - ws[:, :c]
        outs.append(ws[:, c:] + _bdot(pick(qk), v_new, 2, 1))
        state = state * pick(eg_last) + _bdot(pick(kd), v_new, 1, 1)
    s_ref[...] = state.reshape(nb, G_HEADS, G_DK, G_DV)

    for s in range(nb):
        z = gz_ref[s]
        for h in range(G_HEADS):
            o = jnp.concatenate([outs[ci][s * G_HEADS + h] for ci in range(nc)], axis=0)
            o = o * lax.rsqrt(jnp.mean(o * o, axis=-1, keepdims=True) + EPS) * gng_ref[...]
            o_ref[s, :, G_DV * h:G_DV * (h + 1)] = (o * _silu(z[:, G_DV * h:G_DV * (h + 1)])).astype(o_ref.dtype)

    @pl.when(j == nblk - 1)
    def _():
        cnew_ref[...] = tail_ref[...]
        sfin_ref[...] = s_ref[...]


def _gdn(cin, gz, gab, cbuf8, s0, conv_w, alog, dtb, gng, t_valid):
    b, t, _ = cin.shape
    if t_valid == t:
        c = min(GDN_CHUNK, t)
        tb = min(ROW_TILE, t)
        tbv = tb
        nb = GDN_SEQS_PER_STEP if b % GDN_SEQS_PER_STEP == 0 else 1
    else:
        c = tb = t
        tbv = t_valid
        nb = GDN_SHORT_SEQS_PER_STEP if b % GDN_SHORT_SEQS_PER_STEP == 0 else 1
    nblk = t // tb
    blk = lambda w: pl.BlockSpec((nb, tb, w), lambda bb, j: (bb, j, 0))
    per_b = lambda *shape: pl.BlockSpec((nb,) + shape, lambda bb, j: (bb,) + (0,) * len(shape))
    full = lambda a: pl.BlockSpec(a.shape, lambda bb, j: (0,) * a.ndim)
    return pl.pallas_call(
        functools.partial(_gdn_kernel, nb=nb, tb=tb, tbv=tbv, c=c, nblk=nblk),
        grid=(b // nb, nblk),
        in_specs=[blk(CONV_DIM), blk(G_VW), blk(128), per_b(8, CONV_DIM), per_b(G_HEADS, G_DK, G_DV),
                  full(conv_w), full(alog), full(dtb), full(gng)],
        out_specs=[blk(G_VW), per_b(8, CONV_DIM), per_b(G_HEADS, G_DK, G_DV)],
        out_shape=[jax.ShapeDtypeStruct((b, t, G_VW), MXU_DTYPE), jax.ShapeDtypeStruct((b, 8, CONV_DIM), F32),
                   jax.ShapeDtypeStruct((b, G_HEADS, G_DK, G_DV), F32)],
        scratch_shapes=[pltpu.VMEM((nb, 8, CONV_DIM), F32), pltpu.VMEM((nb, G_HEADS, G_DK, G_DV), F32)],
        compiler_params=_cparams("parallel", "arbitrary"),
        name="gdn",
    )(cin, gz, gab, cbuf8, s0, conv_w, alog, dtb, gng)


def _mem_kv_kernel(x_ref, g_ref, w_ref, kg_ref, mkt_ref, mvt_ref):
    x = x_ref[...]
    h = _mx(x * lax.rsqrt(jnp.mean(x * x, axis=-1, keepdims=True) + EPS) * g_ref[...])
    p = jnp.dot(h, w_ref[...], preferred_element_type=F32)
    mk = p[:, :MEM_W]
    mk = mk * lax.rsqrt(_dot(mk * mk, _group_mean_matrix(MEM_W, MEM_HD)) + EPS) * kg_ref[...]
    mkt_ref[...] = mk.T
    mvt_ref[...] = p[:, MEM_W:].T


def _mem_kv(mem, g, w_kv, kg):
    b, m, _ = mem.shape
    full = lambda a: pl.BlockSpec(a.shape, lambda i: (0,) * a.ndim)
    return pl.pallas_call(
        _mem_kv_kernel,
        grid=(b,),
        in_specs=[pl.BlockSpec((None, m, D_MODEL), lambda i: (i, 0, 0)), full(g), full(w_kv), full(kg)],
        out_specs=[pl.BlockSpec((None, MEM_W, m), lambda i: (i, 0, 0))] * 2,
        out_shape=[jax.ShapeDtypeStruct((b, MEM_W, m), F32)] * 2,
        compiler_params=_cparams("parallel"),
        name="mem_kv",
    )(mem, g, w_kv, kg)


def _mix_out_kernel(x_ref, oa_ref, og_ref, woa_ref, wog_ref, g_ref, wq_ref, qg_ref, x1_ref, qm_ref):
    x1 = (x_ref[...] + jnp.dot(oa_ref[...], woa_ref[...], preferred_element_type=F32)
          + jnp.dot(og_ref[...], wog_ref[...], preferred_element_type=F32))
    x1_ref[...] = x1
    h = _mx(x1 * lax.rsqrt(jnp.mean(x1 * x1, axis=-1, keepdims=True) + EPS) * g_ref[...])
    q = jnp.dot(h, wq_ref[...], preferred_element_type=F32)
    q = q * lax.rsqrt(_dot(q * q, _group_mean_matrix(MEM_W, MEM_HD)) + EPS) * qg_ref[...]
    qm_ref[...] = (q * MEM_HD ** -0.5).astype(qm_ref.dtype)


def _mix_out(x2d, oa, og, w_oa, w_og, g, w_mq, qg, tm):
    n = x2d.shape[0]
    row = lambda w: pl.BlockSpec((tm, w), lambda i: (i, 0))
    full = lambda a: pl.BlockSpec(a.shape, lambda i: (0,) * a.ndim)
    return pl.pallas_call(
        _mix_out_kernel,
        grid=(n // tm,),
        in_specs=[row(D_MODEL), row(ATTN_W), row(G_VW), full(w_oa), full(w_og), full(g), full(w_mq), full(qg)],
        out_specs=[row(D_MODEL), row(MEM_W)],
        out_shape=[jax.ShapeDtypeStruct((n, D_MODEL), F32), jax.ShapeDtypeStruct((n, MEM_W), MXU_DTYPE)],
        compiler_params=_cparams("parallel"),
        name="mix_out",
    )(x2d, oa, og, w_oa, w_og, g, w_mq, qg)


def _mem_attn_kernel(q_ref, mkt_ref, mvt_ref, o_ref):
    for h in range(MEM_HEADS):
        cols = slice(MEM_HD * h, MEM_HD * (h + 1))
        s = _bdot(q_ref[:, :, cols], mkt_ref[:, cols, :], 2, 1)
        p = jnp.exp(s - jnp.max(s, axis=2, keepdims=True))
        o = _bdot(p, mvt_ref[:, cols, :], 2, 2) / jnp.sum(p, axis=2, keepdims=True)
        o_ref[:, :, cols] = o.astype(o_ref.dtype)


def _mem_attn(qm, mkt, mvt, tm, nb):
    b, t, _ = qm.shape
    m = mkt.shape[2]
    return pl.pallas_call(
        _mem_attn_kernel,
        grid=(b // nb, t // tm),
        in_specs=[pl.BlockSpec((nb, tm, MEM_W), lambda bb, i: (bb, i, 0)),
                  pl.BlockSpec((nb, MEM_W, m), lambda bb, i: (bb, 0, 0)),
                  pl.BlockSpec((nb, MEM_W, m), lambda bb, i: (bb, 0, 0))],
        out_specs=pl.BlockSpec((nb, tm, MEM_W), lambda bb, i: (bb, i, 0)),
        out_shape=jax.ShapeDtypeStruct((b, t, MEM_W), MXU_DTYPE),
        compiler_params=_cparams("parallel", "arbitrary"),
        name="mem_attn",
    )(qm, mkt, mvt)


def _ffn_kernel(x1_ref, om_ref, wmo_ref, g_ref, wg_ref, wu_ref, wd_ref, y_ref, act_ref, *, fc):
    x2 = x1_ref[...] + jnp.dot(om_ref[...], wmo_ref[...], preferred_element_type=F32)
    h = _mx(x2 * lax.rsqrt(jnp.mean(x2 * x2, axis=-1, keepdims=True) + EPS) * g_ref[...])
    d_ff = wg_ref.shape[1]
    for f0 in range(0, d_ff, fc):
        gate = jnp.dot(h, wg_ref[:, f0:f0 + fc], preferred_element_type=F32)
        up = jnp.dot(h, wu_ref[:, f0:f0 + fc], preferred_element_type=F32)
        act_ref[:, f0:f0 + fc] = (_silu(gate) * up).astype(act_ref.dtype)
    y_ref[...] = x2 + jnp.dot(act_ref[...], wd_ref[...], preferred_element_type=F32)


def _ffn(x1, om, w_mo, g, w_gate, w_up, w_down, tm):
    n = x1.shape[0]
    d_ff = w_gate.shape[1]
    row = lambda w: pl.BlockSpec((tm, w), lambda i: (i, 0))
    full = lambda a: pl.BlockSpec(a.shape, lambda i: (0,) * a.ndim, pipeline_mode=pl.Buffered(1))
    return pl.pallas_call(
        functools.partial(_ffn_kernel, fc=256),
        grid=(n // tm,),
        in_specs=[row(D_MODEL), row(MEM_W), full(w_mo), full(g), full(w_gate), full(w_up), full(w_down)],
        out_specs=row(D_MODEL),
        out_shape=jax.ShapeDtypeStruct((n, D_MODEL), F32),
        scratch_shapes=[pltpu.VMEM((tm, d_ff), MXU_DTYPE)],
        compiler_params=_cparams("parallel"),
        name="ffn",
    )(x1, om, w_mo, g, w_gate, w_up, w_down)


def _prep_weights(attn_norm_g, w_in, q_norm_g, k_norm_g, conv_w, a_log, dt_bias, gdn_norm_g, w_out,
                  xattn_norm_g, w_mq, mq_norm_g, w_mo, ffn_norm_g, w_gate, w_up, w_down):
    lane_pad = lambda a: jnp.pad(a, (0, 128 - a.shape[0]))[None, :]
    return dict(
        attn_g=attn_norm_g[None, :],
        w_dsa=_mx(jnp.pad(w_in[:, :DSA_COLS], ((0, 0), (0, DSA_PAD - DSA_COLS)))),
        w_gdn=_mx(jnp.pad(w_in[:, DSA_COLS:], ((0, 0), (0, GDN_PAD - GDN_COLS)))),
        qg=jnp.tile(q_norm_g, N_HEADS)[None, :],
        kg=jnp.tile(k_norm_g, N_KV)[None, :],
        conv_w=conv_w, alog=lane_pad(a_log), dtb=lane_pad(dt_bias), gng=gdn_norm_g[None, :],
        w_oa=_mx(w_out[:ATTN_W]), w_og=_mx(w_out[ATTN_W:]),
        xattn_g=xattn_norm_g[None, :], w_mq=_mx(w_mq), mqg=jnp.tile(mq_norm_g, MEM_HEADS)[None, :],
        w_mo=_mx(w_mo), ffn_g=ffn_norm_g[None, :], w_gate=_mx(w_gate), w_up=_mx(w_up), w_down=_mx(w_down))


def _trunk_layer(x, mem_kt, mem_vt, conv_buf, ssm0, dsa_sample_fn, w, t_pad):
    b, t, _ = x.shape
    n = b * t
    tm = min(ROW_TILE, n)
    x2d = x.reshape(n, D_MODEL)
    proj = functools.partial(_proj_in, g=w["attn_g"], w_dsa=w["w_dsa"], w_gdn=w["w_gdn"], qg=w["qg"], kg=w["kg"])
    if dsa_sample_fn is None:
        q, kiwi, kt, vt, kit, ktb, va, qi, cin, gz, gab = proj(x, tm=min(ROW_TILE, t), feature_major=True)
        o_attn = _dsa_prompt(q, qi, kiwi, ktb, va, b, t)
        to_tokens = lambda a: jnp.transpose(a.reshape(b, -1, HEAD_DIM, t), (0, 3, 1, 2))
        k_out, v_out, kidx_out = to_tokens(kt), to_tokens(vt), jnp.transpose(kit, (0, 2, 1))
    else:
        q, kiwi, k, v, qi, cin, gz, gab = proj(x2d[None], tm=tm, feature_major=False)
        o_attn = dsa_sample_fn(q, qi, kiwi, k, v)
        k_out, v_out = k.reshape(b, t, N_KV, HEAD_DIM), v.reshape(b, t, N_KV, HEAD_DIM)
        kidx_out = kiwi[:, :IDX_DIM].reshape(b, t, IDX_DIM)
    pad_t = lambda a: jnp.pad(a.reshape(b, t, -1), ((0, 0), (0, t_pad - t), (0, 0)))
    cbuf8 = jnp.pad(conv_buf, ((0, 0), (8 - (CONV_W - 1), 0), (0, 0)))
    o_gdn, cnew8, ssm_new = _gdn(pad_t(cin), pad_t(gz), pad_t(gab), cbuf8, ssm0,
                                 w["conv_w"], w["alog"], w["dtb"], w["gng"], t)
    o_gdn = o_gdn[:, :t].reshape(n, G_VW)
    x1, qm = _mix_out(x2d, o_attn, o_gdn, w["w_oa"], w["w_og"], w["xattn_g"], w["w_mq"], w["mqg"], tm)
    short = t_pad < ROW_TILE and b % MEM_SHORT_SEQS_PER_STEP == 0
    om = _mem_attn(pad_t(qm), mem_kt, mem_vt, min(ROW_TILE, t_pad), MEM_SHORT_SEQS_PER_STEP if short else 1)
    y = _ffn(x1, om[:, :t].reshape(n, MEM_W), w["w_mo"], w["ffn_g"], w["w_gate"], w["w_up"], w["w_down"], tm)
    return y.reshape(b, t, D_MODEL), (k_out, v_out, kidx_out, cnew8[:, 8 - (CONV_W - 1):], ssm_new)


def kernel(x_prompt, x_sample, mem_prompt, cache_k, cache_v, cache_kidx, page_table, state_conv, state_ssm,
           cache_mem_k, cache_mem_v, attn_norm_g, w_in, q_norm_g, k_norm_g, conv_w, a_log, dt_bias, gdn_norm_g,
           w_out, xattn_norm_g, mem_norm_g, w_mq, w_mk, w_mv, mq_norm_g, mk_norm_g, w_mo, ffn_norm_g, w_gate,
           w_up, w_down):
    layer_w = (attn_norm_g, w_in, q_norm_g, k_norm_g, conv_w, a_log, dt_bias, gdn_norm_g, w_out,
               xattn_norm_g, w_mq, mq_norm_g, w_mo, ffn_norm_g, w_gate, w_up, w_down)
    depth = w_in.shape[0]
    bp, tp, _ = x_prompt.shape
    bs, ts, _ = x_sample.shape
    assert ts == 1, "the sample group decodes one token per sequence"
    m = mem_prompt.shape[1]
    feature_major = lambda a: jnp.transpose(a, (0, 2, 3, 1)).reshape(a.shape[0], MEM_W, m)
    token_major = lambda a: jnp.transpose(a.reshape(a.shape[0], MEM_HEADS, MEM_HD, m), (0, 3, 1, 2))
    xp, xs = x_prompt, x_sample
    new_p, new_s = [], []
    for l in range(depth):
        w = _prep_weights(*[a[l] for a in layer_w])
        mkt_p, mvt_p = _mem_kv(mem_prompt, mem_norm_g[l][None, :], _mx(jnp.concatenate([w_mk[l], w_mv[l]], axis=1)),
                               jnp.tile(mk_norm_g[l], MEM_HEADS)[None, :])
        xp, st_p = _trunk_layer(xp, mkt_p, mvt_p, jnp.zeros((bp, CONV_W - 1, CONV_DIM), F32),
                                jnp.zeros((bp, G_HEADS, G_DK, G_DV), F32), None, w, tp)
        dsa_s = lambda q, qi, kiwi, k, v: _dsa_sample(q, qi, kiwi, k, v, cache_k[l], cache_v[l],
                                                      cache_kidx[l], page_table)
        xs, st_s = _trunk_layer(xs, feature_major(cache_mem_k[l]), feature_major(cache_mem_v[l]),
                                state_conv[l], state_ssm[l], dsa_s, w, 8)
        new_p.append(st_p + (token_major(mkt_p), token_major(mvt_p)))
        new_s.append(st_s)
    k_p, v_p, kidx_p, conv_p, ssm_p, memk_p, memv_p = [jnp.stack(z) for z in zip(*new_p)]
    k_s, v_s, kidx_s, conv_s, ssm_s = [jnp.stack(z) for z in zip(*new_s)]
    return (xp, xs, k_p, v_p, kidx_p, conv_p, ssm_p, memk_p, memv_p, k_s, v_s, kidx_s, conv_s, ssm_s)
```

```python
import functools

import jax
import jax.numpy as jnp
from jax import lax
from jax.experimental import pallas as pl
from jax.experimental.pallas import tpu as pltpu

F32 = jnp.float32
MXU_DTYPE = jnp.bfloat16

D_MODEL = 1024
PAGE = 128
N_HEADS, N_KV, HEAD_DIM = 8, 2, 64
ATTN_W, KV_W = N_HEADS * HEAD_DIM, N_KV * HEAD_DIM
IDX_HEADS, IDX_DIM = 4, 64
TOPK_MAX = 256
G_HEADS, G_DK, G_DV = 4, 128, 128
G_KW, G_VW = G_HEADS * G_DK, G_HEADS * G_DV
CONV_W = 4
CONV_DIM = 2 * G_KW + G_VW
GDN_CHUNK = 64
MEM_HEADS, MEM_HD = 4, 64
MEM_W = MEM_HEADS * MEM_HD
DSA_COLS = ATTN_W + 2 * KV_W + IDX_HEADS * IDX_DIM + IDX_DIM + IDX_HEADS
DSA_PAD = 1152
GDN_COLS = 2 * G_KW + 2 * G_VW + 2 * G_HEADS
GDN_PAD = 2176
EPS = 1e-6
NEG = -1e30
INT_MIN = -2 ** 31
KEY_NEG_INF = -2139095041
FLT_MAX = 3.4028234663852886e38
VMEM_LIMIT = 56 * 1024 * 1024
ROW_TILE = 512
DSA_Q_BLOCK = 256
DSA_KEY_CHUNK = 256
BISECT_STEPS_PER_CHECK = 4
GDN_SEQS_PER_STEP = 2
GDN_SHORT_SEQS_PER_STEP = 8
MEM_SHORT_SEQS_PER_STEP = 8
LOG2E = 1.4426950408889634


def _cparams(*sem):
    return pltpu.CompilerParams(dimension_semantics=sem, vmem_limit_bytes=VMEM_LIMIT)


def _mx(a):
    return a.astype(MXU_DTYPE)


def _dot(a, b):
    return jnp.dot(_mx(a), _mx(b), preferred_element_type=F32)


def _dot_nt(a, b):
    return lax.dot_general(_mx(a), _mx(b), (((1,), (1,)), ((), ())), preferred_element_type=F32)


def _bdot(a, b, ca, cb):
    return lax.dot_general(_mx(a), _mx(b), (((ca,), (cb,)), ((0,), (0,))), preferred_element_type=F32)


def _split3(a):
    hi = _mx(a)
    r1 = a - hi.astype(F32)
    mid = _mx(r1)
    lo = _mx(r1 - mid.astype(F32))
    return hi, mid, lo


def _iota2(shape, dim):
    return lax.broadcasted_iota(jnp.int32, shape, dim)


def _tri(n, kind):
    r, c = _iota2((n, n), 0), _iota2((n, n), 1)
    return {"upper": r <= c, "lower": r >= c, "strict_lower": r > c, "eye": r == c}[kind]


def _group_mean_matrix(n, group):
    shift = group.bit_length() - 1
    r, c = _iota2((n, n), 0) >> shift, _iota2((n, n), 1) >> shift
    return jnp.where(r == c, 1.0 / group, 0.0).astype(MXU_DTYPE)


def _silu(x):
    half = 0.5 * x
    return half + half * jnp.tanh(half)


def _key_to_float(key):
    bits = jnp.where(key < 0, key ^ jnp.int32(0x7FFFFFFF), key)
    return jnp.where(key < KEY_NEG_INF, -jnp.inf, lax.bitcast_convert_type(bits, F32))


def _next_candidate(it, base):
    return jnp.where(it == 0, jnp.zeros_like(base), base | lax.shift_left(jnp.int32(1), 31 - it))


def _proj_in_kernel(x_ref, g_ref, wd_ref, wg_ref, qg_ref, kg_ref, *outs, feature_major):
    if feature_major:
        q_ref, kiwi_ref, kt_ref, vt_ref, kit_ref, ktb_ref, va_ref, qi_ref, cin_ref, gz_ref, gab_ref = outs
    else:
        q_ref, kiwi_ref, k_ref, v_ref, qi_ref, cin_ref, gz_ref, gab_ref = outs
    x = x_ref[...]
    h = _mx(x * lax.rsqrt(jnp.mean(x * x, axis=-1, keepdims=True) + EPS) * g_ref[...])
    pd = jnp.dot(h, wd_ref[...], preferred_element_type=F32)
    q = pd[:, :ATTN_W]
    k = pd[:, ATTN_W:ATTN_W + KV_W]
    v = pd[:, ATTN_W + KV_W:ATTN_W + 2 * KV_W]
    qi = pd[:, ATTN_W + 2 * KV_W:ATTN_W + 2 * KV_W + IDX_HEADS * IDX_DIM]
    kiwi = pd[:, 1024:1152]
    m64 = _group_mean_matrix(ATTN_W, HEAD_DIM)
    q = q * lax.rsqrt(_dot(q * q, m64) + EPS) * qg_ref[...]
    k = k * lax.rsqrt(_dot(k * k, m64[:KV_W, :KV_W]) + EPS) * kg_ref[...]
    q_ref[...] = (q * (HEAD_DIM ** -0.5 * LOG2E)).astype(q_ref.dtype)
    kiwi_ref[...] = kiwi
    qi_ref[...] = (qi * IDX_DIM ** -0.5).astype(qi_ref.dtype)
    if feature_major:
        kt, vt, kit = k.T, v.T, kiwi.T[:IDX_DIM]
        kt_ref[...] = kt
        vt_ref[...] = vt
        kit_ref[...] = kit
        ktb_ref[0:KV_W, :] = kt.astype(ktb_ref.dtype)
        ktb_ref[KV_W:KV_W + IDX_DIM, :] = kit.astype(ktb_ref.dtype)
        lane = _iota2(v.shape, 1)
        for g, vg in enumerate((v, jnp.concatenate([v[:, HEAD_DIM:], v[:, :HEAD_DIM]], axis=1))):
            aug = jnp.where(lane < HEAD_DIM, vg, jnp.where(lane == HEAD_DIM, 1.0, 0.0))
            va_ref[:, 128 * g:128 * (g + 1)] = aug.astype(va_ref.dtype)
    else:
        k_ref[...] = k
        v_ref[...] = v
    pg = jnp.dot(h, wg_ref[...], preferred_element_type=F32)
    cin_ref[...] = pg[:, :CONV_DIM]
    gz_ref[...] = pg[:, CONV_DIM:CONV_DIM + G_VW]
    gab_ref[...] = pg[:, CONV_DIM + G_VW:GDN_PAD]


def _proj_in(x, g, w_dsa, w_gdn, qg, kg, tm, feature_major):
    b, t, _ = x.shape
    nt = t // tm
    row = lambda w: pl.BlockSpec((tm, w), lambda bb, i: (bb * nt + i, 0))
    col = lambda w: pl.BlockSpec((None, w, tm), lambda bb, i: (bb, 0, i))
    full = lambda a: pl.BlockSpec(a.shape, lambda bb, i: (0,) * a.ndim)
    rows = lambda w, dt: (row(w), jax.ShapeDtypeStruct((b * t, w), dt))
    cols = lambda w, dt: (col(w), jax.ShapeDtypeStruct((b, w, t), dt))
    outs = [rows(ATTN_W, MXU_DTYPE), rows(128, F32)]
    if feature_major:
        outs += [cols(KV_W, F32), cols(KV_W, F32), cols(IDX_DIM, F32), cols(KV_W + IDX_DIM, MXU_DTYPE),
                 (pl.BlockSpec((None, tm, 256), lambda bb, i: (bb, i, 0)), jax.ShapeDtypeStruct((b, t, 256), MXU_DTYPE))]
    else:
        outs += [rows(KV_W, F32), rows(KV_W, F32)]
    outs += [rows(IDX_HEADS * IDX_DIM, MXU_DTYPE), rows(CONV_DIM, F32), rows(G_VW, F32), rows(128, F32)]
    return pl.pallas_call(
        functools.partial(_proj_in_kernel, feature_major=feature_major),
        grid=(b, nt),
        in_specs=[pl.BlockSpec((None, tm, D_MODEL), lambda bb, i: (bb, i, 0)),
                  full(g), full(w_dsa), full(w_gdn), full(qg), full(kg)],
        out_specs=[o[0] for o in outs],
        out_shape=[o[1] for o in outs],
        compiler_params=_cparams("parallel", "parallel"),
        name="proj_in",
    )(x, g, w_dsa, w_gdn, qg, kg)


def _dsa_prompt_kernel(q_ref, qi_ref, kiwi_ref, kt_ref, va_ref, o_ref, s_ref, bias_ref, *, topk, tq, nkeys, row0):
    kit = kt_ref[KV_W:KV_W + IDX_DIM, :]
    wi = kiwi_ref[:, IDX_DIM:IDX_DIM + IDX_HEADS] * IDX_HEADS ** -0.5
    score = None
    for h in range(IDX_HEADS):
        rel = jnp.maximum(_dot(qi_ref[:, IDX_DIM * h:IDX_DIM * (h + 1)], kit), 0.0)
        term = wi[:, h:h + 1] * rel
        score = term if score is None else score + term
    adm = _iota2((tq, nkeys), 1) <= _iota2((tq, nkeys), 0) + row0
    s_ref[...] = jnp.where(adm, score, -jnp.inf)
    n_cand = lambda r: jnp.minimum(_iota2((r.stop - r.start, 1), 0) + (row0 + r.start + 1), nkeys).astype(F32)
    _topk_bias(s_ref, bias_ref, topk, tq, nkeys, n_cand)
    kc = min(DSA_KEY_CHUNK, nkeys)
    for g in range(N_KV):
        for r in range(N_HEADS // N_KV):
            h = g * (N_HEADS // N_KV) + r
            qh = q_ref[:, HEAD_DIM * h:HEAD_DIM * (h + 1)]
            mm = jnp.full((tq, 128), NEG, F32)
            for c0 in range(0, nkeys, kc):
                ks = slice(c0, c0 + kc)
                sc = _dot(qh, kt_ref[HEAD_DIM * g:HEAD_DIM * (g + 1), ks]) + bias_ref[:, ks]
                s_ref[:, ks] = sc
                for l0 in range(0, kc, 128):
                    mm = jnp.maximum(mm, sc[:, l0:l0 + 128])
            m = jnp.max(mm, axis=1, keepdims=True)
            acc = jnp.zeros((tq, 128), F32)
            for c0 in range(0, nkeys, kc):
                ks = slice(c0, c0 + kc)
                p = jnp.exp2(_mx(s_ref[:, ks] - m))
                acc = acc + jnp.dot(p, va_ref[ks, 128 * g:128 * (g + 1)], preferred_element_type=F32)
            o_ref[:, HEAD_DIM * h:HEAD_DIM * (h + 1)] = (
                acc[:, :HEAD_DIM] / acc[:, HEAD_DIM:HEAD_DIM + 1]).astype(o_ref.dtype)


def _topk_bias(s_ref, bias_ref, topk, tq, nkeys, n_cand):
    halves = 2
    hr = tq // halves
    rows = [slice(p * hr, (p + 1) * hr) for p in range(halves)]
    n_adm = [n_cand(r) for r in rows]

    def pending(cnts):
        worst = [jnp.max(jnp.where(n > float(topk), jnp.abs(c - float(topk)), 0.0)) for n, c in zip(n_adm, cnts)]
        return functools.reduce(jnp.maximum, worst) > 0.0

    def body(state):
        it, _, bases, cnts = state
        bases, cnts = list(bases), list(cnts)
        for j in range(BISECT_STEPS_PER_CHECK):
            for p, r in enumerate(rows):
                cand = _next_candidate(it + j, bases[p])
                cnt = jnp.sum(jnp.where(s_ref[r, :] >= _key_to_float(cand), 1.0, 0.0), axis=1, keepdims=True)
                ok = cnt >= float(topk)
                bases[p] = jnp.where(ok, cand, bases[p])
                cnts[p] = jnp.where(ok, cnt, cnts[p])
        return it + BISECT_STEPS_PER_CHECK, pending(cnts), tuple(bases), tuple(cnts)

    cnts0 = tuple(jnp.full((hr, 1), float(nkeys), F32) for _ in rows)
    bases0 = tuple(jnp.full((hr, 1), INT_MIN, jnp.int32) for _ in rows)
    _, crowded, bases, _ = lax.while_loop(lambda st: (st[0] < 32) & st[1], body,
                                          (jnp.int32(0), pending(cnts0), bases0, cnts0))
    thr = jnp.maximum(_key_to_float(jnp.concatenate(bases, axis=0)), -FLT_MAX)
    thr = jnp.where(jnp.concatenate(n_adm, axis=0) <= float(topk), -FLT_MAX, thr)
    s = s_ref[...]

    @pl.when(jnp.logical_not(crowded))
    def _():
        bias_ref[...] = jnp.where(s >= thr, 0.0, NEG)

    @pl.when(crowded)
    def _():
        gt = s > thr
        tie = s == thr
        need = float(topk) - jnp.sum(jnp.where(gt, 1.0, 0.0), axis=1, keepdims=True)
        tie_f = _mx(jnp.where(tie, 1.0, 0.0))
        upper = _mx(jnp.where(_tri(128, "upper"), 1.0, 0.0))
        off = jnp.zeros((tq, 1), F32)
        parts = []
        for c in range(nkeys // 128):
            pc = jnp.dot(tie_f[:, 128 * c:128 * (c + 1)], upper, preferred_element_type=F32) + off
            off = pc[:, 127:128]
            parts.append(pc)
        rank = jnp.concatenate(parts, axis=1)
        bias_ref[...] = jnp.where(gt | (tie & (rank <= need)), 0.0, NEG)


def _dsa_prompt(q, qi, kiwi, ktb, va, b, t):
    tq = min(DSA_Q_BLOCK, t)
    topk = min(TOPK_MAX, t // 4)
    nq = t // tq
    outs = []
    for i in range(nq):
        nkeys = (i + 1) * tq
        blk = lambda w, i=i: pl.BlockSpec((tq, w), lambda bb: (bb * nq + i, 0))
        outs.append(pl.pallas_call(
            functools.partial(_dsa_prompt_kernel, topk=topk, tq=tq, nkeys=nkeys, row0=i * tq),
            grid=(b,),
            in_specs=[blk(ATTN_W), blk(IDX_HEADS * IDX_DIM), blk(128),
                      pl.BlockSpec((None, KV_W + IDX_DIM, nkeys), lambda bb: (bb, 0, 0)),
                      pl.BlockSpec((None, nkeys, 256), lambda bb: (bb, 0, 0))],
            out_specs=pl.BlockSpec((None, tq, ATTN_W), lambda bb: (bb, 0, 0)),
            out_shape=jax.ShapeDtypeStruct((b, tq, ATTN_W), MXU_DTYPE),
            scratch_shapes=[pltpu.VMEM((tq, nkeys), F32), pltpu.VMEM((tq, nkeys), F32)],
            compiler_params=_cparams("parallel"),
            name=f"dsa_prompt_{i}",
        )(q, qi, kiwi, ktb, va))
    return jnp.concatenate(outs, axis=1).reshape(b * t, ATTN_W)


def _page_gather(pt_ref, pool_hbm, buf, sem, n_pages):
    s = pl.program_id(0)
    slot = lax.rem(s, 2)

    def fetch(seq, into):
        def one(p, carry):
            pltpu.make_async_copy(pool_hbm.at[pt_ref[seq * n_pages + p]], buf.at[into, p], sem.at[into]).start()
            return carry
        lax.fori_loop(0, n_pages, one, 0)

    @pl.when(s == 0)
    def _():
        fetch(0, 0)

    @pl.when(s + 1 < pl.num_programs(0))
    def _():
        fetch(s + 1, 1 - slot)

    pltpu.make_async_copy(pool_hbm.at[pl.ds(0, n_pages)], buf.at[slot], sem.at[slot]).wait()
    return slot


def _dsa_score_kernel(pt_ref, qi_ref, wib_ref, kis_ref, cidx_hbm, sc_ref, buf, sem, *, n_pages):
    slot = _page_gather(pt_ref, cidx_hbm, buf, sem, n_pages)
    qi = qi_ref[...]
    w = wib_ref[...][:, 0:1]
    z = _bdot(jnp.broadcast_to(qi[None], (n_pages,) + qi.shape), buf[slot], 2, 1)
    sc_ref[0:n_pages] = jnp.sum(w * jnp.maximum(z, 0.0), axis=1, keepdims=True)
    z_self = jnp.sum(qi.astype(F32) * _mx(kis_ref[...]).astype(F32), axis=1, keepdims=True)
    s_self = jnp.sum(w * jnp.maximum(z_self, 0.0), axis=0, keepdims=True)
    sc_ref[n_pages] = jnp.where(_iota2((1, 128), 1) == 0, s_self, -jnp.inf)


def _dsa_pick_kernel(s_ref, bias_ref, *, topk, rows, nkeys, n_valid):
    _topk_bias(s_ref, bias_ref, topk, rows, nkeys, lambda r: jnp.full((r.stop - r.start, 1), float(n_valid), F32))


def _dsa_att_kernel(pt_ref, qb_ref, bias_ref, kself_ref, vself_ref, ck_hbm, cv_hbm, o_ref,
                    kbuf, vbuf, ksem, vsem, *, n_pages):
    slot = _page_gather(pt_ref, ck_hbm, kbuf, ksem, n_pages)
    _page_gather(pt_ref, cv_hbm, vbuf, vsem, n_pages)
    qb = qb_ref[...]
    qbb = jnp.broadcast_to(qb[None], (n_pages,) + qb.shape)
    s = _bdot(qbb, kbuf[slot], 2, 1) + bias_ref[0:n_pages]
    self_ok = bias_ref[n_pages][:, 0:1] > -1.0
    s_self = jnp.sum(qb.astype(F32) * _mx(kself_ref[...]).astype(F32), axis=1, keepdims=True)
    s_self = jnp.where(self_ok, s_self, NEG)
    m = jnp.maximum(jnp.max(jnp.max(s, axis=2, keepdims=True), axis=0), s_self)
    p = jnp.where(s > 0.5 * NEG, jnp.exp2(s - m), 0.0)
    p_self = jnp.where(self_ok, jnp.exp2(s_self - m), 0.0)
    l = jnp.sum(jnp.sum(p, axis=2, keepdims=True), axis=0) + p_self
    pv = jnp.sum(_bdot(p, vbuf[slot], 2, 2), axis=0)
    acc = pv + _mx(p_self).astype(F32) * _mx(vself_ref[...]).astype(F32)
    o_ref[...] = (acc / l).astype(o_ref.dtype)


def _dsa_sample(q, qi, kiwi, k_new, v_new, cache_k, cache_v, cache_kidx, page_table):
    b, n_pages = page_table.shape
    n_pool = cache_k.shape[0]
    past = n_pages * PAGE
    topk = min(TOPK_MAX, (past + 1) // 4)
    pt = page_table.reshape(-1)
    ck = jnp.transpose(cache_k, (0, 2, 3, 1)).reshape(n_pool, KV_W, PAGE)
    cv = jnp.transpose(cache_v, (0, 2, 3, 1)).reshape(n_pool, KV_W, PAGE)
    cidx = jnp.transpose(cache_kidx, (0, 2, 1))
    qi8 = jnp.pad(qi.reshape(b, IDX_HEADS, IDX_DIM), ((0, 0), (0, 8 - IDX_HEADS), (0, 0)))
    wib = jnp.pad(kiwi[:, IDX_DIM:IDX_DIM + IDX_HEADS] * IDX_HEADS ** -0.5, ((0, 0), (0, 8 - IDX_HEADS)))
    wib = jnp.broadcast_to(wib[:, :, None], (b, 8, 128))
    kis = kiwi[:, None, :IDX_DIM]
    per_seq = lambda *shape: pl.BlockSpec((None,) + shape, lambda s, pt_: (s,) + (0,) * len(shape))
    hbm = pl.BlockSpec(memory_space=pl.ANY)
    page_buf = lambda rows: [pltpu.VMEM((2, n_pages, rows, PAGE), F32), pltpu.SemaphoreType.DMA((2,))]
    scores = pl.pallas_call(
        functools.partial(_dsa_score_kernel, n_pages=n_pages),
        grid_spec=pltpu.PrefetchScalarGridSpec(
            num_scalar_prefetch=1,
            grid=(b,),
            in_specs=[per_seq(8, IDX_DIM), per_seq(8, 128), per_seq(1, IDX_DIM), hbm],
            out_specs=per_seq(n_pages + 1, 1, 128),
            scratch_shapes=page_buf(IDX_DIM)),
        out_shape=jax.ShapeDtypeStruct((b, n_pages + 1, 1, 128), F32),
        compiler_params=_cparams("arbitrary"),
        name="dsa_sample_score",
    )(pt, qi8, wib, kis, cidx)
    nkeys = (n_pages + 1) * PAGE
    rows = min(b, 64)
    bias = pl.pallas_call(
        functools.partial(_dsa_pick_kernel, topk=topk, rows=rows, nkeys=nkeys, n_valid=past + 1),
        grid=(b // rows,),
        in_specs=[pl.BlockSpec((rows, nkeys), lambda i: (i, 0))],
        out_specs=pl.BlockSpec((rows, nkeys), lambda i: (i, 0)),
        out_shape=jax.ShapeDtypeStruct((b, nkeys), F32),
        compiler_params=_cparams("parallel"),
        name="dsa_sample_pick",
    )(scores.reshape(b, nkeys)).reshape(b, n_pages + 1, 1, PAGE)

    hpg = N_HEADS // N_KV
    qh = q.reshape(b, N_HEADS, HEAD_DIM)
    qb = jnp.concatenate(
        [jnp.pad(qh[:, g * hpg:(g + 1) * hpg], ((0, 0), (0, 0), (g * HEAD_DIM, KV_W - (g + 1) * HEAD_DIM)))
         for g in range(N_KV)], axis=1)
    o8 = pl.pallas_call(
        functools.partial(_dsa_att_kernel, n_pages=n_pages),
        grid_spec=pltpu.PrefetchScalarGridSpec(
            num_scalar_prefetch=1,
            grid=(b,),
            in_specs=[per_seq(8, KV_W), per_seq(n_pages + 1, 1, 128), per_seq(1, KV_W), per_seq(1, KV_W), hbm, hbm],
            out_specs=per_seq(8, KV_W),
            scratch_shapes=[s for pair in zip(page_buf(KV_W), page_buf(KV_W)) for s in pair]),
        out_shape=jax.ShapeDtypeStruct((b, 8, KV_W), F32),
        compiler_params=_cparams("arbitrary"),
        name="dsa_sample_attend",
    )(pt, qb, bias, k_new[:, None, :], v_new[:, None, :], ck, cv)
    o = jnp.concatenate([o8[:, h, (h // hpg) * HEAD_DIM:(h // hpg + 1) * HEAD_DIM] for h in range(N_HEADS)], axis=-1)
    return o.astype(MXU_DTYPE)


def _gdn_kernel(cin_ref, gz_ref, gab_ref, cbuf_ref, s0_ref, cw_ref, alog_ref, dtb_ref, gng_ref,
                o_ref, cnew_ref, sfin_ref, tail_ref, s_ref, *, nb, tb, tbv, c, nblk):
    j = pl.program_id(1)
    nc = tb // c
    ng = nb * G_HEADS

    @pl.when(j == 0)
    def _():
        tail_ref[...] = cbuf_ref[...]
        s_ref[...] = s0_ref[...]

    first = 8 - (CONV_W - 1)
    ys = []
    for s in range(nb):
        full = jnp.concatenate([tail_ref[s], cin_ref[s]], axis=0)
        y = full[first:first + tb] * cw_ref[0:1, :]
        for jj in range(1, CONV_W):
            y = y + full[first + jj:first + jj + tb] * cw_ref[jj:jj + 1, :]
        ys.append(_silu(y))
        tail_ref[s] = full[tbv:tbv + 8]

    def per_group(off, width):
        return jnp.concatenate([y[:, off + width * h:off + width * (h + 1)].reshape(nc, c, width)
                                for y in ys for h in range(G_HEADS)], axis=0)

    q = per_group(0, G_DK)
    k = per_group(G_KW, G_DK)
    v = per_group(2 * G_KW, G_DV)
    ones_dk = jnp.ones((G_DK, G_DK), MXU_DTYPE)

    def sumsq(a):
        a2 = (a * a).reshape(ng * tb, G_DK)
        return jnp.dot(_mx(a2), ones_dk, preferred_element_type=F32).reshape(a.shape)
    q = q * (lax.rsqrt(sumsq(q) + EPS) * G_DK ** -0.5)
    k = k * lax.rsqrt(sumsq(k) + EPS)

    gab = gab_ref[...].reshape(nb * tb, 128)
    xa = gab + dtb_ref[...]
    g_t = -jnp.exp(alog_ref[...]) * (jnp.maximum(xa, 0.0) + jnp.log(1.0 + jnp.exp(-jnp.abs(xa))))
    beta_t = 1.0 / (1.0 + jnp.exp(-gab))
    if tbv < tb:
        live = (_iota2((nb * tb, 128), 0) & (tb - 1)) < tbv
        g_t = jnp.where(live, g_t, 0.0)
        beta_t = jnp.where(live, beta_t, 0.0)
    lower = _tri(c, "lower")
    ones_lower = jnp.broadcast_to(_mx(jnp.where(lower, 1.0, 0.0)), (nb * nc, c, c))
    ones_upper = jnp.broadcast_to(_mx(jnp.where(_tri(c, "upper"), 1.0, 0.0)), (nb * nc, c, c))
    parts = _split3(g_t.reshape(nb * nc, c, 128))
    gc3 = sum(_bdot(ones_lower, p, 2, 1) for p in parts)
    gcr3 = sum(_bdot(p, ones_upper, 1, 1) for p in parts)
    beta3 = beta_t.reshape(nb * nc, c, 128)
    groups = [(slice(s * nc, (s + 1) * nc), h) for s in range(nb) for h in range(G_HEADS)]
    gc = jnp.concatenate([gc3[r, :, h:h + 1] for r, h in groups], axis=0)
    beta = jnp.concatenate([beta3[r, :, G_HEADS + h:G_HEADS + h + 1] for r, h in groups], axis=0)
    gc_row = jnp.concatenate([gcr3[r, h:h + 1, :] for r, h in groups], axis=0)
    g_last = gc[:, c - 1:c, :]
    decay = jnp.where(lower, jnp.exp(jnp.where(lower, gc - gc_row, 0.0)), 0.0)
    kb = k * beta
    vb = v * beta
    eg = jnp.exp(gc)
    kk = _bdot(jnp.concatenate([kb, q], axis=1), k, 2, 2)
    a_mat = jnp.where(_tri(c, "strict_lower"), kk[:, :c] * decay, 0.0)
    qk = jnp.where(lower, kk[:, c:] * decay, 0.0)
    tinv = jnp.where(_tri(c, "eye"), 1.0, 0.0) - a_mat
    apow = a_mat
    width = 2
    while width < c:
        apow2 = _bdot(apow, apow, 2, 1)
        tinv = tinv + _bdot(tinv, apow2, 2, 1)
        apow = apow2
        width *= 2
    uw = _bdot(tinv, jnp.concatenate([vb, kb * eg], axis=2), 2, 1)
    u = uw[:, :, :G_DV]
    wq = _mx(jnp.concatenate([uw[:, :, G_DV:], q * eg], axis=1))
    kd = _mx(k * jnp.exp(g_last - gc))
    eg_last = jnp.exp(g_last)
    qk = _mx(qk)

    state = s_ref[...].reshape(ng, G_DK, G_DV)
    outs = []
    for ci in range(nc):
        pick = lambda a: jnp.concatenate([a[g * nc + ci:g * nc + ci + 1] for g in range(ng)], axis=0)
        ws = _bdot(pick(wq), state, 2, 1)
        v_new = pick(u) - ws[:, :c]
        outs.append(ws[:, c:] + _bdot(pick(qk), v_new, 2, 1))
        state = state * pick(eg_last) + _bdot(pick(kd), v_new, 1, 1)
    s_ref[...] = state.reshape(nb, G_HEADS, G_DK, G_DV)

    for s in range(nb):
        z = gz_ref[s]
        for h in range(G_HEADS):
            o = jnp.concatenate([outs[ci][s * G_HEADS + h] for ci in range(nc)], axis=0)
            o = o * lax.rsqrt(jnp.mean(o * o, axis=-1, keepdims=True) + EPS) * gng_ref[...]
            o_ref[s, :, G_DV * h:G_DV * (h + 1)] = (o * _silu(z[:, G_DV * h:G_DV * (h + 1)])).astype(o_ref.dtype)

    @pl.when(j == nblk - 1)
    def _():
        cnew_ref[...] = tail_ref[...]
        sfin_ref[...] = s_ref[...]


def _gdn(cin, gz, gab, cbuf8, s0, conv_w, alog, dtb, gng, t_valid):
    b, t, _ = cin.shape
    if t_valid == t:
        c = min(GDN_CHUNK, t)
        tb = min(ROW_TILE, t)
        tbv = tb
        nb = GDN_SEQS_PER_STEP if b % GDN_SEQS_PER_STEP == 0 else 1
    else:
        c = tb = t
        tbv = t_valid
        nb = GDN_SHORT_SEQS_PER_STEP if b % GDN_SHORT_SEQS_PER_STEP == 0 else 1
    nblk = t // tb
    blk = lambda w: pl.BlockSpec((nb, tb, w), lambda bb, j: (bb, j, 0))
    per_b = lambda *shape: pl.BlockSpec((nb,) + shape, lambda bb, j: (bb,) + (0,) * len(shape))
    full = lambda a: pl.BlockSpec(a.shape, lambda bb, j: (0,) * a.ndim)
    return pl.pallas_call(
        functools.partial(_gdn_kernel, nb=nb, tb=tb, tbv=tbv, c=c, nblk=nblk),
        grid=(b // nb, nblk),
        in_specs=[blk(CONV_DIM), blk(G_VW), blk(128), per_b(8, CONV_DIM), per_b(G_HEADS, G_DK, G_DV),
                  full(conv_w), full(alog), full(dtb), full(gng)],
        out_specs=[blk(G_VW), per_b(8, CONV_DIM), per_b(G_HEADS, G_DK, G_DV)],
        out_shape=[jax.ShapeDtypeStruct((b, t, G_VW), MXU_DTYPE), jax.ShapeDtypeStruct((b, 8, CONV_DIM), F32),
                   jax.ShapeDtypeStruct((b, G_HEADS, G_DK, G_DV), F32)],
        scratch_shapes=[pltpu.VMEM((nb, 8, CONV_DIM), F32), pltpu.VMEM((nb, G_HEADS, G_DK, G_DV), F32)],
        compiler_params=_cparams("parallel", "arbitrary"),
        name="gdn",
    )(cin, gz, gab, cbuf8, s0, conv_w, alog, dtb, gng)


def _mem_kv_kernel(x_ref, g_ref, w_ref, kg_ref, mkt_ref, mvt_ref):
    x = x_ref[...]
    h = _mx(x * lax.rsqrt(jnp.mean(x * x, axis=-1, keepdims=True) + EPS) * g_ref[...])
    p = jnp.dot(h, w_ref[...], preferred_element_type=F32)
    mk = p[:, :MEM_W]
    mk = mk * lax.rsqrt(_dot(mk * mk, _group_mean_matrix(MEM_W, MEM_HD)) + EPS) * kg_ref[...]
    mkt_ref[...] = mk.T
    mvt_ref[...] = p[:, MEM_W:].T


def _mem_kv(mem, g, w_kv, kg):
    b, m, _ = mem.shape
    full = lambda a: pl.BlockSpec(a.shape, lambda i: (0,) * a.ndim)
    return pl.pallas_call(
        _mem_kv_kernel,
        grid=(b,),
        in_specs=[pl.BlockSpec((None, m, D_MODEL), lambda i: (i, 0, 0)), full(g), full(w_kv), full(kg)],
        out_specs=[pl.BlockSpec((None, MEM_W, m), lambda i: (i, 0, 0))] * 2,
        out_shape=[jax.ShapeDtypeStruct((b, MEM_W, m), F32)] * 2,
        compiler_params=_cparams("parallel"),
        name="mem_kv",
    )(mem, g, w_kv, kg)


def _mix_out_kernel(x_ref, oa_ref, og_ref, woa_ref, wog_ref, g_ref, wq_ref, qg_ref, x1_ref, qm_ref):
    x1 = (x_ref[...] + jnp.dot(oa_ref[...], woa_ref[...], preferred_element_type=F32)
          + jnp.dot(og_ref[...], wog_ref[...], preferred_element_type=F32))
    x1_ref[...] = x1
    h = _mx(x1 * lax.rsqrt(jnp.mean(x1 * x1, axis=-1, keepdims=True) + EPS) * g_ref[...])
    q = jnp.dot(h, wq_ref[...], preferred_element_type=F32)
    q = q * lax.rsqrt(_dot(q * q, _group_mean_matrix(MEM_W, MEM_HD)) + EPS) * qg_ref[...]
    qm_ref[...] = (q * MEM_HD ** -0.5).astype(qm_ref.dtype)


def _mix_out(x2d, oa, og, w_oa, w_og, g, w_mq, qg, tm):
    n = x2d.shape[0]
    row = lambda w: pl.BlockSpec((tm, w), lambda i: (i, 0))
    full = lambda a: pl.BlockSpec(a.shape, lambda i: (0,) * a.ndim)
    return pl.pallas_call(
        _mix_out_kernel,
        grid=(n // tm,),
        in_specs=[row(D_MODEL), row(ATTN_W), row(G_VW), full(w_oa), full(w_og), full(g), full(w_mq), full(qg)],
        out_specs=[row(D_MODEL), row(MEM_W)],
        out_shape=[jax.ShapeDtypeStruct((n, D_MODEL), F32), jax.ShapeDtypeStruct((n, MEM_W), MXU_DTYPE)],
        compiler_params=_cparams("parallel"),
        name="mix_out",
    )(x2d, oa, og, w_oa, w_og, g, w_mq, qg)


def _mem_attn_kernel(q_ref, mkt_ref, mvt_ref, o_ref):
    for h in range(MEM_HEADS):
        cols = slice(MEM_HD * h, MEM_HD * (h + 1))
        s = _bdot(q_ref[:, :, cols], mkt_ref[:, cols, :], 2, 1)
        p = jnp.exp(s - jnp.max(s, axis=2, keepdims=True))
        o = _bdot(p, mvt_ref[:, cols, :], 2, 2) / jnp.sum(p, axis=2, keepdims=True)
        o_ref[:, :, cols] = o.astype(o_ref.dtype)


def _mem_attn(qm, mkt, mvt, tm, nb):
    b, t, _ = qm.shape
    m = mkt.shape[2]
    return pl.pallas_call(
        _mem_attn_kernel,
        grid=(b // nb, t // tm),
        in_specs=[pl.BlockSpec((nb, tm, MEM_W), lambda bb, i: (bb, i, 0)),
                  pl.BlockSpec((nb, MEM_W, m), lambda bb, i: (bb, 0, 0)),
                  pl.BlockSpec((nb, MEM_W, m), lambda bb, i: (bb, 0, 0))],
        out_specs=pl.BlockSpec((nb, tm, MEM_W), lambda bb, i: (bb, i, 0)),
        out_shape=jax.ShapeDtypeStruct((b, t, MEM_W), MXU_DTYPE),
        compiler_params=_cparams("parallel", "arbitrary"),
        name="mem_attn",
    )(qm, mkt, mvt)


def _ffn_kernel(x1_ref, om_ref, wmo_ref, g_ref, wg_ref, wu_ref, wd_ref, y_ref, act_ref, *, fc):
    x2 = x1_ref[...] + jnp.dot(om_ref[...], wmo_ref[...], preferred_element_type=F32)
    h = _mx(x2 * lax.rsqrt(jnp.mean(x2 * x2, axis=-1, keepdims=True) + EPS) * g_ref[...])
    d_ff = wg_ref.shape[1]
    for f0 in range(0, d_ff, fc):
        gate = jnp.dot(h, wg_ref[:, f0:f0 + fc], preferred_element_type=F32)
        up = jnp.dot(h, wu_ref[:, f0:f0 + fc], preferred_element_type=F32)
        act_ref[:, f0:f0 + fc] = (_silu(gate) * up).astype(act_ref.dtype)
    y_ref[...] = x2 + jnp.dot(act_ref[...], wd_ref[...], preferred_element_type=F32)


def _ffn(x1, om, w_mo, g, w_gate, w_up, w_down, tm):
    n = x1.shape[0]
    d_ff = w_gate.shape[1]
    row = lambda w: pl.BlockSpec((tm, w), lambda i: (i, 0))
    full = lambda a: pl.BlockSpec(a.shape, lambda i: (0,) * a.ndim, pipeline_mode=pl.Buffered(1))
    return pl.pallas_call(
        functools.partial(_ffn_kernel, fc=256),
        grid=(n // tm,),
        in_specs=[row(D_MODEL), row(MEM_W), full(w_mo), full(g), full(w_gate), full(w_up), full(w_down)],
        out_specs=row(D_MODEL),
        out_shape=jax.ShapeDtypeStruct((n, D_MODEL), F32),
        scratch_shapes=[pltpu.VMEM((tm, d_ff), MXU_DTYPE)],
        compiler_params=_cparams("parallel"),
        name="ffn",
    )(x1, om, w_mo, g, w_gate, w_up, w_down)


def _prep_weights(attn_norm_g, w_in, q_norm_g, k_norm_g, conv_w, a_log, dt_bias, gdn_norm_g, w_out,
                  xattn_norm_g, w_mq, mq_norm_g, w_mo, ffn_norm_g, w_gate, w_up, w_down):
    lane_pad = lambda a: jnp.pad(a, (0, 128 - a.shape[0]))[None, :]
    return dict(
        attn_g=attn_norm_g[None, :],
        w_dsa=_mx(jnp.pad(w_in[:, :DSA_COLS], ((0, 0), (0, DSA_PAD - DSA_COLS)))),
        w_gdn=_mx(jnp.pad(w_in[:, DSA_COLS:], ((0, 0), (0, GDN_PAD - GDN_COLS)))),
        qg=jnp.tile(q_norm_g, N_HEADS)[None, :],
        kg=jnp.tile(k_norm_g, N_KV)[None, :],
        conv_w=conv_w, alog=lane_pad(a_log), dtb=lane_pad(dt_bias), gng=gdn_norm_g[None, :],
        w_oa=_mx(w_out[:ATTN_W]), w_og=_mx(w_out[ATTN_W:]),
        xattn_g=xattn_norm_g[None, :], w_mq=_mx(w_mq), mqg=jnp.tile(mq_norm_g, MEM_HEADS)[None, :],
        w_mo=_mx(w_mo), ffn_g=ffn_norm_g[None, :], w_gate=_mx(w_gate), w_up=_mx(w_up), w_down=_mx(w_down))


def _trunk_layer(x, mem_kt, mem_vt, conv_buf, ssm0, dsa_sample_fn, w, t_pad):
    b, t, _ = x.shape
    n = b * t
    tm = min(ROW_TILE, n)
    x2d = x.reshape(n, D_MODEL)
    proj = functools.partial(_proj_in, g=w["attn_g"], w_dsa=w["w_dsa"], w_gdn=w["w_gdn"], qg=w["qg"], kg=w["kg"])
    if dsa_sample_fn is None:
        q, kiwi, kt, vt, kit, ktb, va, qi, cin, gz, gab = proj(x, tm=min(ROW_TILE, t), feature_major=True)
        o_attn = _dsa_prompt(q, qi, kiwi, ktb, va, b, t)
        to_tokens = lambda a: jnp.transpose(a.reshape(b, -1, HEAD_DIM, t), (0, 3, 1, 2))
        k_out, v_out, kidx_out = to_tokens(kt), to_tokens(vt), jnp.transpose(kit, (0, 2, 1))
    else:
        q, kiwi, k, v, qi, cin, gz, gab = proj(x2d[None], tm=tm, feature_major=False)
        o_attn = dsa_sample_fn(q, qi, kiwi, k, v)
        k_out, v_out = k.reshape(b, t, N_KV, HEAD_DIM), v.reshape(b, t, N_KV, HEAD_DIM)
        kidx_out = kiwi[:, :IDX_DIM].reshape(b, t, IDX_DIM)
    pad_t = lambda a: jnp.pad(a.reshape(b, t, -1), ((0, 0), (0, t_pad - t), (0, 0)))
    cbuf8 = jnp.pad(conv_buf, ((0, 0), (8 - (CONV_W - 1), 0), (0, 0)))
    o_gdn, cnew8, ssm_new = _gdn(pad_t(cin), pad_t(gz), pad_t(gab), cbuf8, ssm0,
                                 w["conv_w"], w["alog"], w["dtb"], w["gng"], t)
    o_gdn = o_gdn[:, :t].reshape(n, G_VW)
    x1, qm = _mix_out(x2d, o_attn, o_gdn, w["w_oa"], w["w_og"], w["xattn_g"], w["w_mq"], w["mqg"], tm)
    short = t_pad < ROW_TILE and b % MEM_SHORT_SEQS_PER_STEP == 0
    om = _mem_attn(pad_t(qm), mem_kt, mem_vt, min(ROW_TILE, t_pad), MEM_SHORT_SEQS_PER_STEP if short else 1)
    y = _ffn(x1, om[:, :t].reshape(n, MEM_W), w["w_mo"], w["ffn_g"], w["w_gate"], w["w_up"], w["w_down"], tm)
    return y.reshape(b, t, D_MODEL), (k_out, v_out, kidx_out, cnew8[:, 8 - (CONV_W - 1):], ssm_new)


def kernel(x_prompt, x_sample, mem_prompt, cache_k, cache_v, cache_kidx, page_table, state_conv, state_ssm,
           cache_mem_k, cache_mem_v, attn_norm_g, w_in, q_norm_g, k_norm_g, conv_w, a_log, dt_bias, gdn_norm_g,
           w_out, xattn_norm_g, mem_norm_g, w_mq, w_mk, w_mv, mq_norm_g, mk_norm_g, w_mo, ffn_norm_g, w_gate,
           w_up, w_down):
    layer_w = (attn_norm_g, w_in, q_norm_g, k_norm_g, conv_w, a_log, dt_bias, gdn_norm_g, w_out,
               xattn_norm_g, w_mq, mq_norm_g, w_mo, ffn_norm_g, w_gate, w_up, w_down)
    depth = w_in.shape[0]
    bp, tp, _ = x_prompt.shape
    bs, ts, _ = x_sample.shape
    assert ts == 1, "the sample group decodes one token per sequence"
    m = mem_prompt.shape[1]
    feature_major = lambda a: jnp.transpose(a, (0, 2, 3, 1)).reshape(a.shape[0], MEM_W, m)
    token_major = lambda a: jnp.transpose(a.reshape(a.shape[0], MEM_HEADS, MEM_HD, m), (0, 3, 1, 2))
    xp, xs = x_prompt, x_sample
    new_p, new_s = [], []
    for l in range(depth):
        w = _prep_weights(*[a[l] for a in layer_w])
        mkt_p, mvt_p = _mem_kv(mem_prompt, mem_norm_g[l][None, :], _mx(jnp.concatenate([w_mk[l], w_mv[l]], axis=1)),
                               jnp.tile(mk_norm_g[l], MEM_HEADS)[None, :])
        xp, st_p = _trunk_layer(xp, mkt_p, mvt_p, jnp.zeros((bp, CONV_W - 1, CONV_DIM), F32),
                                jnp.zeros((bp, G_HEADS, G_DK, G_DV), F32), None, w, tp)
        dsa_s = lambda q, qi, kiwi, k, v: _dsa_sample(q, qi, kiwi, k, v, cache_k[l], cache_v[l],
                                                      cache_kidx[l], page_table)
        xs, st_s = _trunk_layer(xs, feature_major(cache_mem_k[l]), feature_major(cache_mem_v[l]),
                                state_conv[l], state_ssm[l], dsa_s, w, 8)
        new_p.append(st_p + (token_major(mkt_p), token_major(mvt_p)))
        new_s.append(st_s)
    k_p, v_p, kidx_p, conv_p, ssm_p, memk_p, memv_p = [jnp.stack(z) for z in zip(*new_p)]
    k_s, v_s, kidx_s, conv_s, ssm_s = [jnp.stack(z) for z in zip(*new_s)]
    return (xp, xs, k_p, v_p, kidx_p, conv_p, ssm_p, memk_p, memv_p, k_s, v_s, kidx_s, conv_s, ssm_s)
```

```python
import functools

import jax
import jax.numpy as jnp
from jax import lax
from jax.experimental import pallas as pl
from jax.experimental.pallas import tpu as pltpu

F32 = jnp.float32
MXU_DTYPE = jnp.bfloat16

D_MODEL = 1024
PAGE = 128
N_HEADS, N_KV, HEAD_DIM = 8, 2, 64
ATTN_W, KV_W = N_HEADS * HEAD_DIM, N_KV * HEAD_DIM
IDX_HEADS, IDX_DIM = 4, 64
TOPK_MAX = 256
G_HEADS, G_DK, G_DV = 4, 128, 128
G_KW, G_VW = G_HEADS * G_DK, G_HEADS * G_DV
CONV_W = 4
CONV_DIM = 2 * G_KW + G_VW
GDN_CHUNK = 64
MEM_HEADS, MEM_HD = 4, 64
MEM_W = MEM_HEADS * MEM_HD
DSA_COLS = ATTN_W + 2 * KV_W + IDX_HEADS * IDX_DIM + IDX_DIM + IDX_HEADS
DSA_PAD = 1152
GDN_COLS = 2 * G_KW + 2 * G_VW + 2 * G_HEADS
GDN_PAD = 2176
EPS = 1e-6
NEG = -1e30
INT_MIN = -2 ** 31
KEY_NEG_INF = -2139095041
FLT_MAX = 3.4028234663852886e38
VMEM_LIMIT = 56 * 1024 * 1024
ROW_TILE = 512
DSA_Q_BLOCK = 256
DSA_KEY_CHUNK = 256
BISECT_STEPS_PER_CHECK = 4
GDN_SEQS_PER_STEP = 2
GDN_SHORT_SEQS_PER_STEP = 8
MEM_SHORT_SEQS_PER_STEP = 8
LOG2E = 1.4426950408889634


def _cparams(*sem):
    return pltpu.CompilerParams(dimension_semantics=sem, vmem_limit_bytes=VMEM_LIMIT)


def _mx(a):
    return a.astype(MXU_DTYPE)


def _dot(a, b):
    return jnp.dot(_mx(a), _mx(b), preferred_element_type=F32)


def _dot_nt(a, b):
    return lax.dot_general(_mx(a), _mx(b), (((1,), (1,)), ((), ())), preferred_element_type=F32)


def _bdot(a, b, ca, cb):
    return lax.dot_general(_mx(a), _mx(b), (((ca,), (cb,)), ((0,), (0,))), preferred_element_type=F32)


def _split3(a):
    hi = _mx(a)
    r1 = a - hi.astype(F32)
    mid = _mx(r1)
    lo = _mx(r1 - mid.astype(F32))
    return hi, mid, lo


def _iota2(shape, dim):
    return lax.broadcasted_iota(jnp.int32, shape, dim)


def _tri(n, kind):
    r, c = _iota2((n, n), 0), _iota2((n, n), 1)
    return {"upper": r <= c, "lower": r >= c, "strict_lower": r > c, "eye": r == c}[kind]


def _group_mean_matrix(n, group):
    shift = group.bit_length() - 1
    r, c = _iota2((n, n), 0) >> shift, _iota2((n, n), 1) >> shift
    return jnp.where(r == c, 1.0 / group, 0.0).astype(MXU_DTYPE)


def _silu(x):
    half = 0.5 * x
    return half + half * jnp.tanh(half)


def _key_to_float(key):
    bits = jnp.where(key < 0, key ^ jnp.int32(0x7FFFFFFF), key)
    return jnp.where(key < KEY_NEG_INF, -jnp.inf, lax.bitcast_convert_type(bits, F32))


def _next_candidate(it, base):
    return jnp.where(it == 0, jnp.zeros_like(base), base | lax.shift_left(jnp.int32(1), 31 - it))


def _proj_in_kernel(x_ref, g_ref, wd_ref, wg_ref, qg_ref, kg_ref, *outs, feature_major):
    if feature_major:
        q_ref, kiwi_ref, kt_ref, vt_ref, kit_ref, ktb_ref, va_ref, qi_ref, cin_ref, gz_ref, gab_ref = outs
    else:
        q_ref, kiwi_ref, k_ref, v_ref, qi_ref, cin_ref, gz_ref, gab_ref = outs
    x = x_ref[...]
    h = _mx(x * lax.rsqrt(jnp.mean(x * x, axis=-1, keepdims=True) + EPS) * g_ref[...])
    pd = jnp.dot(h, wd_ref[...], preferred_element_type=F32)
    q = pd[:, :ATTN_W]
    k = pd[:, ATTN_W:ATTN_W + KV_W]
    v = pd[:, ATTN_W + KV_W:ATTN_W + 2 * KV_W]
    qi = pd[:, ATTN_W + 2 * KV_W:ATTN_W + 2 * KV_W + IDX_HEADS * IDX_DIM]
    kiwi = pd[:, 1024:1152]
    m64 = _group_mean_matrix(ATTN_W, HEAD_DIM)
    q = q * lax.rsqrt(_dot(q * q, m64) + EPS) * qg_ref[...]
    k = k * lax.rsqrt(_dot(k * k, m64[:KV_W, :KV_W]) + EPS) * kg_ref[...]
    q_ref[...] = (q * (HEAD_DIM ** -0.5 * LOG2E)).astype(q_ref.dtype)
    kiwi_ref[...] = kiwi
    qi_ref[...] = (qi * IDX_DIM ** -0.5).astype(qi_ref.dtype)
    if feature_major:
        kt, vt, kit = k.T, v.T, kiwi.T[:IDX_DIM]
        kt_ref[...] = kt
        vt_ref[...] = vt
        kit_ref[...] = kit
        ktb_ref[0:KV_W, :] = kt.astype(ktb_ref.dtype)
        ktb_ref[KV_W:KV_W + IDX_DIM, :] = kit.astype(ktb_ref.dtype)
        lane = _iota2(v.shape, 1)
        for g, vg in enumerate((v, jnp.concatenate([v[:, HEAD_DIM:], v[:, :HEAD_DIM]], axis=1))):
            aug = jnp.where(lane < HEAD_DIM, vg, jnp.where(lane == HEAD_DIM, 1.0, 0.0))
            va_ref[:, 128 * g:128 * (g + 1)] = aug.astype(va_ref.dtype)
    else:
        k_ref[...] = k
        v_ref[...] = v
    pg = jnp.dot(h, wg_ref[...], preferred_element_type=F32)
    cin_ref[...] = pg[:, :CONV_DIM]
    gz_ref[...] = pg[:, CONV_DIM:CONV_DIM + G_VW]
    gab_ref[...] = pg[:, CONV_DIM + G_VW:GDN_PAD]


def _proj_in(x, g, w_dsa, w_gdn, qg, kg, tm, feature_major):
    b, t, _ = x.shape
    nt = t // tm
    row = lambda w: pl.BlockSpec((tm, w), lambda bb, i: (bb * nt + i, 0))
    col = lambda w: pl.BlockSpec((None, w, tm), lambda bb, i: (bb, 0, i))
    full = lambda a: pl.BlockSpec(a.shape, lambda bb, i: (0,) * a.ndim)
    rows = lambda w, dt: (row(w), jax.ShapeDtypeStruct((b * t, w), dt))
    cols = lambda w, dt: (col(w), jax.ShapeDtypeStruct((b, w, t), dt))
    outs = [rows(ATTN_W, MXU_DTYPE), rows(128, F32)]
    if feature_major:
        outs += [cols(KV_W, F32), cols(KV_W, F32), cols(IDX_DIM, F32), cols(KV_W + IDX_DIM, MXU_DTYPE),
                 (pl.BlockSpec((None, tm, 256), lambda bb, i: (bb, i, 0)), jax.ShapeDtypeStruct((b, t, 256), MXU_DTYPE))]
    else:
        outs += [rows(KV_W, F32), rows(KV_W, F32)]
    outs += [rows(IDX_HEADS * IDX_DIM, MXU_DTYPE), rows(CONV_DIM, F32), rows(G_VW, F32), rows(128, F32)]
    return pl.pallas_call(
        functools.partial(_proj_in_kernel, feature_major=feature_major),
        grid=(b, nt),
        in_specs=[pl.BlockSpec((None, tm, D_MODEL), lambda bb, i: (bb, i, 0)),
                  full(g), full(w_dsa), full(w_gdn), full(qg), full(kg)],
        out_specs=[o[0] for o in outs],
        out_shape=[o[1] for o in outs],
        compiler_params=_cparams("parallel", "parallel"),
        name="proj_in",
    )(x, g, w_dsa, w_gdn, qg, kg)


def _dsa_prompt_kernel(q_ref, qi_ref, kiwi_ref, kt_ref, va_ref, o_ref, s_ref, bias_ref, *, topk, tq, nkeys, row0):
    kit = kt_ref[KV_W:KV_W + IDX_DIM, :]
    wi = kiwi_ref[:, IDX_DIM:IDX_DIM + IDX_HEADS] * IDX_HEADS ** -0.5
    score = None
    for h in range(IDX_HEADS):
        rel = jnp.maximum(_dot(qi_ref[:, IDX_DIM * h:IDX_DIM * (h + 1)], kit), 0.0)
        term = wi[:, h:h + 1] * rel
        score = term if score is None else score + term
    adm = _iota2((tq, nkeys), 1) <= _iota2((tq, nkeys), 0) + row0
    s_ref[...] = jnp.where(adm, score, -jnp.inf)
    n_cand = lambda r: jnp.minimum(_iota2((r.stop - r.start, 1), 0) + (row0 + r.start + 1), nkeys).astype(F32)
    _topk_bias(s_ref, bias_ref, topk, tq, nkeys, n_cand)
    kc = min(DSA_KEY_CHUNK, nkeys)
    for g in range(N_KV):
        for r in range(N_HEADS // N_KV):
            h = g * (N_HEADS // N_KV) + r
            qh = q_ref[:, HEAD_DIM * h:HEAD_DIM * (h + 1)]
            mm = jnp.full((tq, 128), NEG, F32)
            for c0 in range(0, nkeys, kc):
                ks = slice(c0, c0 + kc)
                sc = _dot(qh, kt_ref[HEAD_DIM * g:HEAD_DIM * (g + 1), ks]) + bias_ref[:, ks]
                s_ref[:, ks] = sc
                for l0 in range(0, kc, 128):
                    mm = jnp.maximum(mm, sc[:, l0:l0 + 128])
            m = jnp.max(mm, axis=1, keepdims=True)
            acc = jnp.zeros((tq, 128), F32)
            for c0 in range(0, nkeys, kc):
                ks = slice(c0, c0 + kc)
                p = jnp.exp2(_mx(s_ref[:, ks] - m))
                acc = acc + jnp.dot(p, va_ref[ks, 128 * g:128 * (g + 1)], preferred_element_type=F32)
            o_ref[:, HEAD_DIM * h:HEAD_DIM * (h + 1)] = (
                acc[:, :HEAD_DIM] / acc[:, HEAD_DIM:HEAD_DIM + 1]).astype(o_ref.dtype)


def _topk_bias(s_ref, bias_ref, topk, tq, nkeys, n_cand):
    halves = 2
    hr = tq // halves
    rows = [slice(p * hr, (p + 1) * hr) for p in range(halves)]
    n_adm = [n_cand(r) for r in rows]

    def pending(cnts):
        worst = [jnp.max(jnp.where(n > float(topk), jnp.abs(c - float(topk)), 0.0)) for n, c in zip(n_adm, cnts)]
        return functools.reduce(jnp.maximum, worst) > 0.0

    def body(state):
        it, _, bases, cnts = state
        bases, cnts = list(bases), list(cnts)
        for j in range(BISECT_STEPS_PER_CHECK):
            for p, r in enumerate(rows):
                cand = _next_candidate(it + j, bases[p])
                cnt = jnp.sum(jnp.where(s_ref[r, :] >= _key_to_float(cand), 1.0, 0.0), axis=1, keepdims=True)
                ok = cnt >= float(topk)
                bases[p] = jnp.where(ok, cand, bases[p])
                cnts[p] = jnp.where(ok, cnt, cnts[p])
        return it + BISECT_STEPS_PER_CHECK, pending(cnts), tuple(bases), tuple(cnts)

    cnts0 = tuple(jnp.full((hr, 1), float(nkeys), F32) for _ in rows)
    bases0 = tuple(jnp.full((hr, 1), INT_MIN, jnp.int32) for _ in rows)
    _, crowded, bases, _ = lax.while_loop(lambda st: (st[0] < 32) & st[1], body,
                                          (jnp.int32(0), pending(cnts0), bases0, cnts0))
    thr = jnp.maximum(_key_to_float(jnp.concatenate(bases, axis=0)), -FLT_MAX)
    thr = jnp.where(jnp.concatenate(n_adm, axis=0) <= float(topk), -FLT_MAX, thr)
    s = s_ref[...]

    @pl.when(jnp.logical_not(crowded))
    def _():
        bias_ref[...] = jnp.where(s >= thr, 0.0, NEG)

    @pl.when(crowded)
    def _():
        gt = s > thr
        tie = s == thr
        need = float(topk) - jnp.sum(jnp.where(gt, 1.0, 0.0), axis=1, keepdims=True)
        tie_f = _mx(jnp.where(tie, 1.0, 0.0))
        upper = _mx(jnp.where(_tri(128, "upper"), 1.0, 0.0))
        off = jnp.zeros((tq, 1), F32)
        parts = []
        for c in range(nkeys // 128):
            pc = jnp.dot(tie_f[:, 128 * c:128 * (c + 1)], upper, preferred_element_type=F32) + off
            off = pc[:, 127:128]
            parts.append(pc)
        rank = jnp.concatenate(parts, axis=1)
        bias_ref[...] = jnp.where(gt | (tie & (rank <= need)), 0.0, NEG)


def _dsa_prompt(q, qi, kiwi, ktb, va, b, t):
    tq = min(DSA_Q_BLOCK, t)
    topk = min(TOPK_MAX, t // 4)
    nq = t // tq
    outs = []
    for i in range(nq):
        nkeys = (i + 1) * tq
        blk = lambda w, i=i: pl.BlockSpec((tq, w), lambda bb: (bb * nq + i, 0))
        outs.append(pl.pallas_call(
            functools.partial(_dsa_prompt_kernel, topk=topk, tq=tq, nkeys=nkeys, row0=i * tq),
            grid=(b,),
            in_specs=[blk(ATTN_W), blk(IDX_HEADS * IDX_DIM), blk(128),
                      pl.BlockSpec((None, KV_W + IDX_DIM, nkeys), lambda bb: (bb, 0, 0)),
                      pl.BlockSpec((None, nkeys, 256), lambda bb: (bb, 0, 0))],
            out_specs=pl.BlockSpec((None, tq, ATTN_W), lambda bb: (bb, 0, 0)),
            out_shape=jax.ShapeDtypeStruct((b, tq, ATTN_W), MXU_DTYPE),
            scratch_shapes=[pltpu.VMEM((tq, nkeys), F32), pltpu.VMEM((tq, nkeys), F32)],
            compiler_params=_cparams("parallel"),
            name=f"dsa_prompt_{i}",
        )(q, qi, kiwi, ktb, va))
    return jnp.concatenate(outs, axis=1).reshape(b * t, ATTN_W)


def _page_gather(pt_ref, pool_hbm, buf, sem, n_pages, priority=0):
    s = pl.program_id(0)
    slot = lax.rem(s, 2)

    def fetch(seq, into):
        def one(p, carry):
            pltpu.make_async_copy(pool_hbm.at[pt_ref[seq * n_pages + p]], buf.at[into, p],
                                  sem.at[into]).start(priority=priority)
            return carry
        lax.fori_loop(0, n_pages, one, 0)

    @pl.when(s == 0)
    def _():
        fetch(0, 0)

    @pl.when(s + 1 < pl.num_programs(0))
    def _():
        fetch(s + 1, 1 - slot)

    pltpu.make_async_copy(pool_hbm.at[pl.ds(0, n_pages)], buf.at[slot], sem.at[slot]).wait()
    return slot


def _dsa_score_kernel(pt_ref, qi_ref, wib_ref, kis_ref, cidx_hbm, sc_ref, buf, sem, *, n_pages):
    slot = _page_gather(pt_ref, cidx_hbm, buf, sem, n_pages)
    qi = qi_ref[...]
    w = wib_ref[...][:, 0:1]
    z = _bdot(jnp.broadcast_to(qi[None], (n_pages,) + qi.shape), buf[slot], 2, 1)
    sc_ref[0:n_pages] = jnp.sum(w * jnp.maximum(z, 0.0), axis=1, keepdims=True)
    z_self = jnp.sum(qi.astype(F32) * _mx(kis_ref[...]).astype(F32), axis=1, keepdims=True)
    s_self = jnp.sum(w * jnp.maximum(z_self, 0.0), axis=0, keepdims=True)
    sc_ref[n_pages] = jnp.where(_iota2((1, 128), 1) == 0, s_self, -jnp.inf)


def _dsa_pick_kernel(s_ref, bias_ref, *, topk, rows, nkeys, n_valid):
    _topk_bias(s_ref, bias_ref, topk, rows, nkeys, lambda r: jnp.full((r.stop - r.start, 1), float(n_valid), F32))


def _dsa_att_kernel(pt_ref, qb_ref, bias_ref, kself_ref, vself_ref, ck_hbm, cv_hbm, o_ref,
                    kbuf, vbuf, ksem, vsem, *, n_pages):
    slot = _page_gather(pt_ref, ck_hbm, kbuf, ksem, n_pages)
    _page_gather(pt_ref, cv_hbm, vbuf, vsem, n_pages, priority=1)
    qb = qb_ref[...]
    qbb = jnp.broadcast_to(qb[None], (n_pages,) + qb.shape)
    s = _bdot(qbb, kbuf[slot], 2, 1) + bias_ref[0:n_pages]
    self_ok = bias_ref[n_pages][:, 0:1] > -1.0
    s_self = jnp.sum(qb.astype(F32) * _mx(kself_ref[...]).astype(F32), axis=1, keepdims=True)
    s_self = jnp.where(self_ok, s_self, NEG)
    m = jnp.maximum(jnp.max(jnp.max(s, axis=2, keepdims=True), axis=0), s_self)
    p = jnp.where(s > 0.5 * NEG, jnp.exp2(s - m), 0.0)
    p_self = jnp.where(self_ok, jnp.exp2(s_self - m), 0.0)
    l = jnp.sum(jnp.sum(p, axis=2, keepdims=True), axis=0) + p_self
    pv = jnp.sum(_bdot(p, vbuf[slot], 2, 2), axis=0)
    acc = pv + _mx(p_self).astype(F32) * _mx(vself_ref[...]).astype(F32)
    o_ref[...] = (acc / l).astype(o_ref.dtype)


def _dsa_sample(q, qi, kiwi, k_new, v_new, cache_k, cache_v, cache_kidx, page_table):
    b, n_pages = page_table.shape
    n_pool = cache_k.shape[0]
    past = n_pages * PAGE
    topk = min(TOPK_MAX, (past + 1) // 4)
    pt = page_table.reshape(-1)
    ck = jnp.transpose(cache_k, (0, 2, 3, 1)).reshape(n_pool, KV_W, PAGE)
    cv = jnp.transpose(cache_v, (0, 2, 3, 1)).reshape(n_pool, KV_W, PAGE)
    cidx = jnp.transpose(cache_kidx, (0, 2, 1))
    qi8 = jnp.pad(qi.reshape(b, IDX_HEADS, IDX_DIM), ((0, 0), (0, 8 - IDX_HEADS), (0, 0)))
    wib = jnp.pad(kiwi[:, IDX_DIM:IDX_DIM + IDX_HEADS] * IDX_HEADS ** -0.5, ((0, 0), (0, 8 - IDX_HEADS)))
    wib = jnp.broadcast_to(wib[:, :, None], (b, 8, 128))
    kis = kiwi[:, None, :IDX_DIM]
    per_seq = lambda *shape: pl.BlockSpec((None,) + shape, lambda s, pt_: (s,) + (0,) * len(shape))
    hbm = pl.BlockSpec(memory_space=pl.ANY)
    page_buf = lambda rows: [pltpu.VMEM((2, n_pages, rows, PAGE), F32), pltpu.SemaphoreType.DMA((2,))]
    scores = pl.pallas_call(
        functools.partial(_dsa_score_kernel, n_pages=n_pages),
        grid_spec=pltpu.PrefetchScalarGridSpec(
            num_scalar_prefetch=1,
            grid=(b,),
            in_specs=[per_seq(8, IDX_DIM), per_seq(8, 128), per_seq(1, IDX_DIM), hbm],
            out_specs=per_seq(n_pages + 1, 1, 128),
            scratch_shapes=page_buf(IDX_DIM)),
        out_shape=jax.ShapeDtypeStruct((b, n_pages + 1, 1, 128), F32),
        compiler_params=_cparams("arbitrary"),
        name="dsa_sample_score",
    )(pt, qi8, wib, kis, cidx)
    nkeys = (n_pages + 1) * PAGE
    rows = min(b, 64)
    bias = pl.pallas_call(
        functools.partial(_dsa_pick_kernel, topk=topk, rows=rows, nkeys=nkeys, n_valid=past + 1),
        grid=(b // rows,),
        in_specs=[pl.BlockSpec((rows, nkeys), lambda i: (i, 0))],
        out_specs=pl.BlockSpec((rows, nkeys), lambda i: (i, 0)),
        out_shape=jax.ShapeDtypeStruct((b, nkeys), F32),
        compiler_params=_cparams("parallel"),
        name="dsa_sample_pick",
    )(scores.reshape(b, nkeys)).reshape(b, n_pages + 1, 1, PAGE)

    hpg = N_HEADS // N_KV
    qh = q.reshape(b, N_HEADS, HEAD_DIM)
    qb = jnp.concatenate(
        [jnp.pad(qh[:, g * hpg:(g + 1) * hpg], ((0, 0), (0, 0), (g * HEAD_DIM, KV_W - (g + 1) * HEAD_DIM)))
         for g in range(N_KV)], axis=1)
    o8 = pl.pallas_call(
        functools.partial(_dsa_att_kernel, n_pages=n_pages),
        grid_spec=pltpu.PrefetchScalarGridSpec(
            num_scalar_prefetch=1,
            grid=(b,),
            in_specs=[per_seq(8, KV_W), per_seq(n_pages + 1, 1, 128), per_seq(1, KV_W), per_seq(1, KV_W), hbm, hbm],
            out_specs=per_seq(8, KV_W),
            scratch_shapes=[s for pair in zip(page_buf(KV_W), page_buf(KV_W)) for s in pair]),
        out_shape=jax.ShapeDtypeStruct((b, 8, KV_W), F32),
        compiler_params=_cparams("arbitrary"),
        name="dsa_sample_attend",
    )(pt, qb, bias, k_new[:, None, :], v_new[:, None, :], ck, cv)
    o = jnp.concatenate([o8[:, h, (h // hpg) * HEAD_DIM:(h // hpg + 1) * HEAD_DIM] for h in range(N_HEADS)], axis=-1)
    return o.astype(MXU_DTYPE)


def _gdn_kernel(cin_ref, gz_ref, gab_ref, cbuf_ref, s0_ref, cw_ref, alog_ref, dtb_ref, gng_ref,
                o_ref, cnew_ref, sfin_ref, tail_ref, s_ref, *, nb, tb, tbv, c, nblk):
    j = pl.program_id(1)
    nc = tb // c
    ng = nb * G_HEADS

    @pl.when(j == 0)
    def _():
        tail_ref[...] = cbuf_ref[...]
        s_ref[...] = s0_ref[...]

    first = 8 - (CONV_W - 1)
    ys = []
    for s in range(nb):
        full = jnp.concatenate([tail_ref[s], cin_ref[s]], axis=0)
        y = full[first:first + tb] * cw_ref[0:1, :]
        for jj in range(1, CONV_W):
            y = y + full[first + jj:first + jj + tb] * cw_ref[jj:jj + 1, :]
        ys.append(_silu(y))
        tail_ref[s] = full[tbv:tbv + 8]

    def per_group(off, width):
        return jnp.concatenate([y[:, off + width * h:off + width * (h + 1)].reshape(nc, c, width)
                                for y in ys for h in range(G_HEADS)], axis=0)

    q = per_group(0, G_DK)
    k = per_group(G_KW, G_DK)
    v = per_group(2 * G_KW, G_DV)
    ones_dk = jnp.ones((G_DK, G_DK), MXU_DTYPE)

    def sumsq(a):
        a2 = (a * a).reshape(ng * tb, G_DK)
        return jnp.dot(_mx(a2), ones_dk, preferred_element_type=F32).reshape(a.shape)
    q = q * (lax.rsqrt(sumsq(q) + EPS) * G_DK ** -0.5)
    k = k * lax.rsqrt(sumsq(k) + EPS)

    gab = gab_ref[...].reshape(nb * tb, 128)
    xa = gab + dtb_ref[...]
    g_t = -jnp.exp(alog_ref[...]) * (jnp.maximum(xa, 0.0) + jnp.log(1.0 + jnp.exp(-jnp.abs(xa))))
    beta_t = 1.0 / (1.0 + jnp.exp(-gab))
    if tbv < tb:
        live = (_iota2((nb * tb, 128), 0) & (tb - 1)) < tbv
        g_t = jnp.where(live, g_t, 0.0)
        beta_t = jnp.where(live, beta_t, 0.0)
    lower = _tri(c, "lower")
    ones_lower = jnp.broadcast_to(_mx(jnp.where(lower, 1.0, 0.0)), (nb * nc, c, c))
    ones_upper = jnp.broadcast_to(_mx(jnp.where(_tri(c, "upper"), 1.0, 0.0)), (nb * nc, c, c))
    parts = _split3(g_t.reshape(nb * nc, c, 128))
    gc3 = sum(_bdot(ones_lower, p, 2, 1) for p in parts)
    gcr3 = sum(_bdot(p, ones_upper, 1, 1) for p in parts)
    beta3 = beta_t.reshape(nb * nc, c, 128)
    groups = [(slice(s * nc, (s + 1) * nc), h) for s in range(nb) for h in range(G_HEADS)]
    gc = jnp.concatenate([gc3[r, :, h:h + 1] for r, h in groups], axis=0)
    beta = jnp.concatenate([beta3[r, :, G_HEADS + h:G_HEADS + h + 1] for r, h in groups], axis=0)
    gc_row = jnp.concatenate([gcr3[r, h:h + 1, :] for r, h in groups], axis=0)
    g_last = gc[:, c - 1:c, :]
    decay = jnp.where(lower, jnp.exp(jnp.where(lower, gc - gc_row, 0.0)), 0.0)
    kb = k * beta
    vb = v * beta
    eg = jnp.exp(gc)
    kk = _bdot(jnp.concatenate([kb, q], axis=1), k, 2, 2)
    a_mat = jnp.where(_tri(c, "strict_lower"), kk[:, :c] * decay, 0.0)
    qk = jnp.where(lower, kk[:, c:] * decay, 0.0)
    tinv = jnp.where(_tri(c, "eye"), 1.0, 0.0) - a_mat
    apow = a_mat
    width = 2
    while width < c:
        apow2 = _bdot(apow, apow, 2, 1)
        tinv = tinv + _bdot(tinv, apow2, 2, 1)
        apow = apow2
        width *= 2
    uw = _bdot(tinv, jnp.concatenate([vb, kb * eg], axis=2), 2, 1)
    u = uw[:, :, :G_DV]
    wq = _mx(jnp.concatenate([uw[:, :, G_DV:], q * eg], axis=1))
    kd = _mx(k * jnp.exp(g_last - gc))
    eg_last = jnp.exp(g_last)
    qk = _mx(qk)

    state = s_ref[...].reshape(ng, G_DK, G_DV)
    outs = []
    for ci in range(nc):
        pick = lambda a: jnp.concatenate([a[g * nc + ci:g * nc + ci + 1] for g in range(ng)], axis=0)
        ws = _bdot(pick(wq), state, 2, 1)
        v_new = pick(u) - ws[:, :c]
        outs.append(ws[:, c:] + _bdot(pick(qk), v_new, 2, 1))
        state = state * pick(eg_last) + _bdot(pick(kd), v_new, 1, 1)
    s_ref[...] = state.reshape(nb, G_HEADS, G_DK, G_DV)

    for s in range(nb):
        z = gz_ref[s]
        for h in range(G_HEADS):
            o = jnp.concatenate([outs[ci][s * G_HEADS + h] for ci in range(nc)], axis=0)
            o = o * lax.rsqrt(jnp.mean(o * o, axis=-1, keepdims=True) + EPS) * gng_ref[...]
            o_ref[s, :, G_DV * h:G_DV * (h + 1)] = (o * _silu(z[:, G_DV * h:G_DV * (h + 1)])).astype(o_ref.dtype)

    @pl.when(j == nblk - 1)
    def _():
        cnew_ref[...] = tail_ref[...]
        sfin_ref[...] = s_ref[...]


def _gdn(cin, gz, gab, cbuf8, s0, conv_w, alog, dtb, gng, t_valid):
    b, t, _ = cin.shape
    if t_valid == t:
        c = min(GDN_CHUNK, t)
        tb = min(ROW_TILE, t)
        tbv = tb
        nb = GDN_SEQS_PER_STEP if b % GDN_SEQS_PER_STEP == 0 else 1
    else:
        c = tb = t
        tbv = t_valid
        nb = GDN_SHORT_SEQS_PER_STEP if b % GDN_SHORT_SEQS_PER_STEP == 0 else 1
    nblk = t // tb
    blk = lambda w: pl.BlockSpec((nb, tb, w), lambda bb, j: (bb, j, 0))
    per_b = lambda *shape: pl.BlockSpec((nb,) + shape, lambda bb, j: (bb,) + (0,) * len(shape))
    full = lambda a: pl.BlockSpec(a.shape, lambda bb, j: (0,) * a.ndim)
    return pl.pallas_call(
        functools.partial(_gdn_kernel, nb=nb, tb=tb, tbv=tbv, c=c, nblk=nblk),
        grid=(b // nb, nblk),
        in_specs=[blk(CONV_DIM), blk(G_VW), blk(128), per_b(8, CONV_DIM), per_b(G_HEADS, G_DK, G_DV),
                  full(conv_w), full(alog), full(dtb), full(gng)],
        out_specs=[blk(G_VW), per_b(8, CONV_DIM), per_b(G_HEADS, G_DK, G_DV)],
        out_shape=[jax.ShapeDtypeStruct((b, t, G_VW), MXU_DTYPE), jax.ShapeDtypeStruct((b, 8, CONV_DIM), F32),
                   jax.ShapeDtypeStruct((b, G_HEADS, G_DK, G_DV), F32)],
        scratch_shapes=[pltpu.VMEM((nb, 8, CONV_DIM), F32), pltpu.VMEM((nb, G_HEADS, G_DK, G_DV), F32)],
        compiler_params=_cparams("parallel", "arbitrary"),
        name="gdn",
    )(cin, gz, gab, cbuf8, s0, conv_w, alog, dtb, gng)


def _mem_kv_kernel(x_ref, g_ref, w_ref, kg_ref, mkt_ref, mvt_ref):
    x = x_ref[...]
    h = _mx(x * lax.rsqrt(jnp.mean(x * x, axis=-1, keepdims=True) + EPS) * g_ref[...])
    p = jnp.dot(h, w_ref[...], preferred_element_type=F32)
    mk = p[:, :MEM_W]
    mk = mk * lax.rsqrt(_dot(mk * mk, _group_mean_matrix(MEM_W, MEM_HD)) + EPS) * kg_ref[...]
    mkt_ref[...] = mk.T
    mvt_ref[...] = p[:, MEM_W:].T


def _mem_kv(mem, g, w_kv, kg):
    b, m, _ = mem.shape
    full = lambda a: pl.BlockSpec(a.shape, lambda i: (0,) * a.ndim)
    return pl.pallas_call(
        _mem_kv_kernel,
        grid=(b,),
        in_specs=[pl.BlockSpec((None, m, D_MODEL), lambda i: (i, 0, 0)), full(g), full(w_kv), full(kg)],
        out_specs=[pl.BlockSpec((None, MEM_W, m), lambda i: (i, 0, 0))] * 2,
        out_shape=[jax.ShapeDtypeStruct((b, MEM_W, m), F32)] * 2,
        compiler_params=_cparams("parallel"),
        name="mem_kv",
    )(mem, g, w_kv, kg)


def _mix_out_kernel(x_ref, oa_ref, og_ref, woa_ref, wog_ref, g_ref, wq_ref, qg_ref, x1_ref, qm_ref):
    x1 = (x_ref[...] + jnp.dot(oa_ref[...], woa_ref[...], preferred_element_type=F32)
          + jnp.dot(og_ref[...], wog_ref[...], preferred_element_type=F32))
    x1_ref[...] = x1
    h = _mx(x1 * lax.rsqrt(jnp.mean(x1 * x1, axis=-1, keepdims=True) + EPS) * g_ref[...])
    q = jnp.dot(h, wq_ref[...], preferred_element_type=F32)
    q = q * lax.rsqrt(_dot(q * q, _group_mean_matrix(MEM_W, MEM_HD)) + EPS) * qg_ref[...]
    qm_ref[...] = (q * MEM_HD ** -0.5).astype(qm_ref.dtype)


def _mix_out(x2d, oa, og, w_oa, w_og, g, w_mq, qg, tm):
    n = x2d.shape[0]
    row = lambda w: pl.BlockSpec((tm, w), lambda i: (i, 0))
    full = lambda a: pl.BlockSpec(a.shape, lambda i: (0,) * a.ndim)
    return pl.pallas_call(
        _mix_out_kernel,
        grid=(n // tm,),
        in_specs=[row(D_MODEL), row(ATTN_W), row(G_VW), full(w_oa), full(w_og), full(g), full(w_mq), full(qg)],
        out_specs=[row(D_MODEL), row(MEM_W)],
        out_shape=[jax.ShapeDtypeStruct((n, D_MODEL), F32), jax.ShapeDtypeStruct((n, MEM_W), MXU_DTYPE)],
        compiler_params=_cparams("parallel"),
        name="mix_out",
    )(x2d, oa, og, w_oa, w_og, g, w_mq, qg)


def _mem_attn_kernel(q_ref, mkt_ref, mvt_ref, o_ref):
    for h in range(MEM_HEADS):
        cols = slice(MEM_HD * h, MEM_HD * (h + 1))
        s = _bdot(q_ref[:, :, cols], mkt_ref[:, cols, :], 2, 1)
        p = jnp.exp(s - jnp.max(s, axis=2, keepdims=True))
        o = _bdot(p, mvt_ref[:, cols, :], 2, 2) / jnp.sum(p, axis=2, keepdims=True)
        o_ref[:, :, cols] = o.astype(o_ref.dtype)


def _mem_attn(qm, mkt, mvt, tm, nb):
    b, t, _ = qm.shape
    m = mkt.shape[2]
    return pl.pallas_call(
        _mem_attn_kernel,
        grid=(b // nb, t // tm),
        in_specs=[pl.BlockSpec((nb, tm, MEM_W), lambda bb, i: (bb, i, 0)),
                  pl.BlockSpec((nb, MEM_W, m), lambda bb, i: (bb, 0, 0)),
                  pl.BlockSpec((nb, MEM_W, m), lambda bb, i: (bb, 0, 0))],
        out_specs=pl.BlockSpec((nb, tm, MEM_W), lambda bb, i: (bb, i, 0)),
        out_shape=jax.ShapeDtypeStruct((b, t, MEM_W), MXU_DTYPE),
        compiler_params=_cparams("parallel", "arbitrary"),
        name="mem_attn",
    )(qm, mkt, mvt)


def _ffn_kernel(x1_ref, om_ref, wmo_ref, g_ref, wg_ref, wu_ref, wd_ref, y_ref, act_ref, *, fc):
    x2 = x1_ref[...] + jnp.dot(om_ref[...], wmo_ref[...], preferred_element_type=F32)
    h = _mx(x2 * lax.rsqrt(jnp.mean(x2 * x2, axis=-1, keepdims=True) + EPS) * g_ref[...])
    d_ff = wg_ref.shape[1]
    for f0 in range(0, d_ff, fc):
        gate = jnp.dot(h, wg_ref[:, f0:f0 + fc], preferred_element_type=F32)
        up = jnp.dot(h, wu_ref[:, f0:f0 + fc], preferred_element_type=F32)
        act_ref[:, f0:f0 + fc] = (_silu(gate) * up).astype(act_ref.dtype)
    y_ref[...] = x2 + jnp.dot(act_ref[...], wd_ref[...], preferred_element_type=F32)


def _ffn(x1, om, w_mo, g, w_gate, w_up, w_down, tm):
    n = x1.shape[0]
    d_ff = w_gate.shape[1]
    row = lambda w: pl.BlockSpec((tm, w), lambda i: (i, 0))
    full = lambda a: pl.BlockSpec(a.shape, lambda i: (0,) * a.ndim, pipeline_mode=pl.Buffered(1))
    return pl.pallas_call(
        functools.partial(_ffn_kernel, fc=256),
        grid=(n // tm,),
        in_specs=[row(D_MODEL), row(MEM_W), full(w_mo), full(g), full(w_gate), full(w_up), full(w_down)],
        out_specs=row(D_MODEL),
        out_shape=jax.ShapeDtypeStruct((n, D_MODEL), F32),
        scratch_shapes=[pltpu.VMEM((tm, d_ff), MXU_DTYPE)],
        compiler_params=_cparams("parallel"),
        name="ffn",
    )(x1, om, w_mo, g, w_gate, w_up, w_down)


def _prep_weights(attn_norm_g, w_in, q_norm_g, k_norm_g, conv_w, a_log, dt_bias, gdn_norm_g, w_out,
                  xattn_norm_g, w_mq, mq_norm_g, w_mo, ffn_norm_g, w_gate, w_up, w_down):
    lane_pad = lambda a: jnp.pad(a, (0, 128 - a.shape[0]))[None, :]
    return dict(
        attn_g=attn_norm_g[None, :],
        w_dsa=_mx(jnp.pad(w_in[:, :DSA_COLS], ((0, 0), (0, DSA_PAD - DSA_COLS)))),
        w_gdn=_mx(jnp.pad(w_in[:, DSA_COLS:], ((0, 0), (0, GDN_PAD - GDN_COLS)))),
        qg=jnp.tile(q_norm_g, N_HEADS)[None, :],
        kg=jnp.tile(k_norm_g, N_KV)[None, :],
        conv_w=conv_w, alog=lane_pad(a_log), dtb=lane_pad(dt_bias), gng=gdn_norm_g[None, :],
        w_oa=_mx(w_out[:ATTN_W]), w_og=_mx(w_out[ATTN_W:]),
        xattn_g=xattn_norm_g[None, :], w_mq=_mx(w_mq), mqg=jnp.tile(mq_norm_g, MEM_HEADS)[None, :],
        w_mo=_mx(w_mo), ffn_g=ffn_norm_g[None, :], w_gate=_mx(w_gate), w_up=_mx(w_up), w_down=_mx(w_down))


def _trunk_layer(x, mem_kt, mem_vt, conv_buf, ssm0, dsa_sample_fn, w, t_pad):
    b, t, _ = x.shape
    n = b * t
    tm = min(ROW_TILE, n)
    x2d = x.reshape(n, D_MODEL)
    proj = functools.partial(_proj_in, g=w["attn_g"], w_dsa=w["w_dsa"], w_gdn=w["w_gdn"], qg=w["qg"], kg=w["kg"])
    if dsa_sample_fn is None:
        q, kiwi, kt, vt, kit, ktb, va, qi, cin, gz, gab = proj(x, tm=min(ROW_TILE, t), feature_major=True)
        o_attn = _dsa_prompt(q, qi, kiwi, ktb, va, b, t)
        to_tokens = lambda a: jnp.transpose(a.reshape(b, -1, HEAD_DIM, t), (0, 3, 1, 2))
        k_out, v_out, kidx_out = to_tokens(kt), to_tokens(vt), jnp.transpose(kit, (0, 2, 1))
    else:
        q, kiwi, k, v, qi, cin, gz, gab = proj(x2d[None], tm=tm, feature_major=False)
        o_attn = dsa_sample_fn(q, qi, kiwi, k, v)
        k_out, v_out = k.reshape(b, t, N_KV, HEAD_DIM), v.reshape(b, t, N_KV, HEAD_DIM)
        kidx_out = kiwi[:, :IDX_DIM].reshape(b, t, IDX_DIM)
    pad_t = lambda a: jnp.pad(a.reshape(b, t, -1), ((0, 0), (0, t_pad - t), (0, 0)))
    cbuf8 = jnp.pad(conv_buf, ((0, 0), (8 - (CONV_W - 1), 0), (0, 0)))
    o_gdn, cnew8, ssm_new = _gdn(pad_t(cin), pad_t(gz), pad_t(gab), cbuf8, ssm0,
                                 w["conv_w"], w["alog"], w["dtb"], w["gng"], t)
    o_gdn = o_gdn[:, :t].reshape(n, G_VW)
    x1, qm = _mix_out(x2d, o_attn, o_gdn, w["w_oa"], w["w_og"], w["xattn_g"], w["w_mq"], w["mqg"], tm)
    short = t_pad < ROW_TILE and b % MEM_SHORT_SEQS_PER_STEP == 0
    om = _mem_attn(pad_t(qm), mem_kt, mem_vt, min(ROW_TILE, t_pad), MEM_SHORT_SEQS_PER_STEP if short else 1)
    y = _ffn(x1, om[:, :t].reshape(n, MEM_W), w["w_mo"], w["ffn_g"], w["w_gate"], w["w_up"], w["w_down"], tm)
    return y.reshape(b, t, D_MODEL), (k_out, v_out, kidx_out, cnew8[:, 8 - (CONV_W - 1):], ssm_new)


def kernel(x_prompt, x_sample, mem_prompt, cache_k, cache_v, cache_kidx, page_table, state_conv, state_ssm,
           cache_mem_k, cache_mem_v, attn_norm_g, w_in, q_norm_g, k_norm_g, conv_w, a_log, dt_bias, gdn_norm_g,
           w_out, xattn_norm_g, mem_norm_g, w_mq, w_mk, w_mv, mq_norm_g, mk_norm_g, w_mo, ffn_norm_g, w_gate,
           w_up, w_down):
    layer_w = (attn_norm_g, w_in, q_norm_g, k_norm_g, conv_w, a_log, dt_bias, gdn_norm_g, w_out,
               xattn_norm_g, w_mq, mq_norm_g, w_mo, ffn_norm_g, w_gate, w_up, w_down)
    depth = w_in.shape[0]
    bp, tp, _ = x_prompt.shape
    bs, ts, _ = x_sample.shape
    assert ts == 1, "the sample group decodes one token per sequence"
    m = mem_prompt.shape[1]
    feature_major = lambda a: jnp.transpose(a, (0, 2, 3, 1)).reshape(a.shape[0], MEM_W, m)
    token_major = lambda a: jnp.transpose(a.reshape(a.shape[0], MEM_HEADS, MEM_HD, m), (0, 3, 1, 2))
    xp, xs = x_prompt, x_sample
    new_p, new_s = [], []
    for l in range(depth):
        w = _prep_weights(*[a[l] for a in layer_w])
        mkt_p, mvt_p = _mem_kv(mem_prompt, mem_norm_g[l][None, :], _mx(jnp.concatenate([w_mk[l], w_mv[l]], axis=1)),
                               jnp.tile(mk_norm_g[l], MEM_HEADS)[None, :])
        xp, st_p = _trunk_layer(xp, mkt_p, mvt_p, jnp.zeros((bp, CONV_W - 1, CONV_DIM), F32),
                                jnp.zeros((bp, G_HEADS, G_DK, G_DV), F32), None, w, tp)
        dsa_s = lambda q, qi, kiwi, k, v: _dsa_sample(q, qi, kiwi, k, v, cache_k[l], cache_v[l],
                                                      cache_kidx[l], page_table)
        xs, st_s = _trunk_layer(xs, feature_major(cache_mem_k[l]), feature_major(cache_mem_v[l]),
                                state_conv[l], state_ssm[l], dsa_s, w, 8)
        new_p.append(st_p + (token_major(mkt_p), token_major(mvt_p)))
        new_s.append(st_s)
    k_p, v_p, kidx_p, conv_p, ssm_p, memk_p, memv_p = [jnp.stack(z) for z in zip(*new_p)]
    k_s, v_s, kidx_s, conv_s, ssm_s = [jnp.stack(z) for z in zip(*new_s)]
    return (xp, xs, k_p, v_p, kidx_p, conv_p, ssm_p, memk_p, memv_p, k_s, v_s, kidx_s, conv_s, ssm_s)
```
